```python
import jax, jax.numpy as jnp
from jax import lax
import numpy as np

D_MODEL = 2048
BATCH = 2
SEQ = 16384
DEPTH = 1

HEAD_DIM = 128
NSA_WIDTH = D_MODEL // 2
NSA_HEADS = NSA_WIDTH // HEAD_DIM
NSA_KV_GROUPS = 2
NSA_REP = NSA_HEADS // NSA_KV_GROUPS
KV_WIDTH = NSA_KV_GROUPS * HEAD_DIM
N_BRANCH = 3
CMP_LEN = 32
CMP_STRIDE = 16
SEL_LEN = 64
SEL_TOPK = 16
WINDOW = 512
FORCE_SCORE = 1e4
GMLP_WIDTH = D_MODEL - NSA_WIDTH
GMLP_GROUP_DIM = 128
GMLP_GROUPS = GMLP_WIDTH // GMLP_GROUP_DIM
GMLP_CHUNK = 128
D_FF = 5632
CONV_WIDTH = 3
ROPE_THETA = 10000.0
EPS = 1e-6
Q_BLOCK = 128
NEG_INF = -1e30
Q_END = NSA_WIDTH
KV_END = Q_END + 6 * KV_WIDTH
GATE_END = KV_END + NSA_HEADS * N_BRANCH
U_END = GATE_END + GMLP_WIDTH
D_IN = U_END + GMLP_WIDTH

kernel_name = "hybrid_nsa_gmlp_convglu_layer"


def rms_norm(x, g):
    x32 = x.astype(jnp.float32)
    y = x32 * lax.rsqrt(jnp.mean(x32 * x32, axis=-1, keepdims=True) + EPS)
    return (y * g.astype(jnp.float32)).astype(x.dtype)


def rope(x, pos):
    half = HEAD_DIM // 2
    inv = ROPE_THETA ** (-2.0 * jnp.arange(half, dtype=jnp.float32) / HEAD_DIM)
    ang = pos.astype(jnp.float32)[..., None] * inv
    cos = jnp.cos(ang)[:, :, None, :]
    sin = jnp.sin(ang)[:, :, None, :]
    x32 = x.astype(jnp.float32)
    x1, x2 = x32[..., :half], x32[..., half:]
    return jnp.concatenate([x1 * cos - x2 * sin, x2 * cos + x1 * sin], axis=-1).astype(x.dtype)


def masked_softmax(s, mask):
    p = jax.nn.softmax(jnp.where(mask, s, NEG_INF), axis=-1)
    return jnp.where(mask, p, 0.0)


def compress_blocks(k, blk_idx, pe, w):
    kb = k[:, blk_idx] + pe[None, None, :, None, :]
    return jnp.einsum('bnlgd,lde->bnge', kb, w)


def nsa_mixer(q, k_cmp, v_cmp, k_sel, v_sel, k_win, v_win, gates, positions,
              cmp_pe_k, cmp_w_k, cmp_pe_v, cmp_w_v):
    B, S, H, Dh = q.shape
    G, R = NSA_KV_GROUPS, NSA_REP
    n_cmp = (S - CMP_LEN) // CMP_STRIDE + 1
    n_sel = S // SEL_LEN
    top_k = min(SEL_TOPK, n_sel)
    n_qb = S // Q_BLOCK
    scale = HEAD_DIM ** -0.5

    q = rope(q, positions)
    c_start = jnp.arange(n_cmp) * CMP_STRIDE
    c_end = c_start + CMP_LEN - 1
    blk_idx = c_start[:, None] + jnp.arange(CMP_LEN)[None, :]
    kc = rope(compress_blocks(k_cmp, blk_idx, cmp_pe_k, cmp_w_k), positions[:, c_end])
    vc = compress_blocks(v_cmp, blk_idx, cmp_pe_v, cmp_w_v)
    js = jnp.arange(n_sel)[None, :] * SEL_LEN
    agg = ((c_start[:, None] < js + SEL_LEN) & (c_start[:, None] + CMP_LEN > js)).astype(jnp.float32)
    ksr = rope(k_sel, positions).reshape(B, n_sel, SEL_LEN, G, Dh).transpose(0, 3, 1, 2, 4)
    vsr = v_sel.reshape(B, n_sel, SEL_LEN, G, Dh).transpose(0, 3, 1, 2, 4)
    pad = ((0, 0), (WINDOW, 0), (0, 0), (0, 0))
    kwp = jnp.pad(rope(k_win, positions), pad)
    vwp = jnp.pad(v_win, pad)

    q_blocks = q.reshape(B, n_qb, Q_BLOCK, G, R, Dh).transpose(1, 0, 2, 3, 4, 5)
    g_blocks = gates.reshape(B, n_qb, Q_BLOCK, H, N_BRANCH).transpose(1, 0, 2, 3, 4)
    b_ix = jnp.arange(B)[:, None, None, None]
    g_ix = jnp.arange(G)[None, :, None, None]
    sel_tok = jnp.arange(SEL_LEN)
    win_off = jnp.arange(Q_BLOCK + WINDOW)
    sel_j = jnp.arange(n_sel)

    def block(args):
        qb, qblk, gblk = args
        t = qb * Q_BLOCK + jnp.arange(Q_BLOCK)
        s = jnp.einsum('bqgrd,bngd->bgrqn', qblk, kc).astype(jnp.float32) * scale
        p_c = masked_softmax(s, c_end[None, :] <= t[:, None])
        o_c = jnp.einsum('bgrqn,bngd->bqgrd', p_c.astype(vc.dtype), vc)
        imp = jnp.einsum('bgrqn,nj->bgqj', p_c, agg)
        cur = (t // SEL_LEN)[:, None]
        forced = (sel_j[None, :] == 0) | (sel_j[None, :] == cur) | (sel_j[None, :] == cur - 1)
        imp = jnp.where(forced, FORCE_SCORE, jnp.where(sel_j[None, :] <= cur, imp, -1.0))
        _, idx = lax.top_k(imp, top_k)
        ks = ksr[b_ix, g_ix, idx]
        vs = vsr[b_ix, g_ix, idx]
        s = jnp.einsum('bqgrd,bgqkld->bgrqkl', qblk, ks).astype(jnp.float32) * scale
        tok = idx[..., None] * SEL_LEN + sel_tok
        m = (tok <= t[None, None, :, None, None]).reshape(B, G, 1, Q_BLOCK, top_k * SEL_LEN)
        p_s = masked_softmax(s.reshape(B, G, R, Q_BLOCK, top_k * SEL_LEN), m)
        o_s = jnp.einsum('bgrqm,bgqmd->bqgrd', p_s.astype(vs.dtype),
                         vs.reshape(B, G, Q_BLOCK, top_k * SEL_LEN, Dh))
        kw = lax.dynamic_slice_in_dim(kwp, qb * Q_BLOCK, Q_BLOCK + WINDOW, axis=1)
        vw = lax.dynamic_slice_in_dim(vwp, qb * Q_BLOCK, Q_BLOCK + WINDOW, axis=1)
        kpos = qb * Q_BLOCK - WINDOW + win_off
        diff = t[:, None] - kpos[None, :]
        m = (kpos[None, :] >= 0) & (diff >= 0) & (diff < WINDOW)
        s = jnp.einsum('bqgrd,bkgd->bgrqk', qblk, kw).astype(jnp.float32) * scale
        p_w = masked_softmax(s, m)
        o_w = jnp.einsum('bgrqk,bkgd->bqgrd', p_w.astype(vw.dtype), vw)
        gg = jax.nn.sigmoid(gblk.astype(jnp.float32)).reshape(B, Q_BLOCK, G, R, N_BRANCH)
        o = gg[..., 0:1] * o_c + gg[..., 1:2] * o_s + gg[..., 2:3] * o_w
        return o.astype(qblk.dtype)

    out = lax.map(block, (jnp.arange(n_qb), q_blocks, g_blocks))
    return out.transpose(1, 0, 2, 3, 4, 5).reshape(B, S, H * Dh)


def gmlp_mixer(u, v, v_norm_g, w_s, b_s):
    B, S, _ = u.shape
    n_ch = S // GMLP_CHUNK
    u = jax.nn.gelu(u, approximate=False)
    v = jax.nn.gelu(v, approximate=False)
    vg = rms_norm(v.reshape(B, S, GMLP_GROUPS, GMLP_GROUP_DIM), v_norm_g)
    vg = vg.reshape(B, n_ch, GMLP_CHUNK, GMLP_GROUPS, GMLP_GROUP_DIM)
    tril = jnp.tril(jnp.ones((GMLP_CHUNK, GMLP_CHUNK), dtype=bool))
    ws = jnp.where(tril[None], w_s, 0.0).astype(vg.dtype)
    mixed = jnp.einsum('hts,bnshc->bnthc', ws, vg) + b_s.T[None, None, :, :, None]
    ug = u.reshape(B, n_ch, GMLP_CHUNK, GMLP_GROUPS, GMLP_GROUP_DIM)
    return (ug * mixed).reshape(B, S, GMLP_WIDTH)


def conv_glu_ffn(y, w_up, conv_w, conv_b, w_down):
    a = y @ w_up
    S = a.shape[1]
    c = conv_b
    for i in range(CONV_WIDTH):
        shift = CONV_WIDTH - 1 - i
        tap = a if shift == 0 else jnp.pad(a[:, :S - shift], ((0, 0), (shift, 0), (0, 0)))
        c = c + conv_w[i] * tap
    gate, up = jnp.split(c, 2, axis=-1)
    return (jax.nn.silu(gate) * up) @ w_down


def setup_inputs(seed: int = 0) -> dict:
    key = jax.random.key(seed)
    ks = jax.random.split(key, 24)
    f32 = jnp.float32
    nrm = lambda k, shape, s: jax.random.normal(k, shape, f32) * s
    L = DEPTH
    return {
        "x": jax.random.normal(ks[0], (BATCH, SEQ, D_MODEL), f32),
        "positions": jnp.broadcast_to(jnp.arange(SEQ, dtype=jnp.int32), (BATCH, SEQ)),
        "norm1_g": 1.0 + nrm(ks[1], (L, D_MODEL), 0.02),
        "w_in": nrm(ks[2], (L, D_MODEL, D_IN), D_MODEL ** -0.5),
        "cmp_pe_k": nrm(ks[3], (L, CMP_LEN, HEAD_DIM), 0.1),
        "cmp_w_k": nrm(ks[4], (L, CMP_LEN, HEAD_DIM, HEAD_DIM), (CMP_LEN * HEAD_DIM) ** -0.5),
        "cmp_pe_v": nrm(ks[5], (L, CMP_LEN, HEAD_DIM), 0.1),
        "cmp_w_v": nrm(ks[6], (L, CMP_LEN, HEAD_DIM, HEAD_DIM), (CMP_LEN * HEAD_DIM) ** -0.5),
        "gmlp_norm_g": 1.0 + nrm(ks[7], (L, GMLP_GROUPS, GMLP_GROUP_DIM), 0.02),
        "gmlp_w_s": nrm(ks[8], (L, GMLP_GROUPS, GMLP_CHUNK, GMLP_CHUNK), GMLP_CHUNK ** -0.5),
        "gmlp_b_s": 1.0 + nrm(ks[9], (L, GMLP_GROUPS, GMLP_CHUNK), 0.1),
        "nsa_out_g": 1.0 + nrm(ks[10], (L, NSA_WIDTH), 0.02),
        "gmlp_out_g": 1.0 + nrm(ks[11], (L, GMLP_WIDTH), 0.02),
        "w_out": nrm(ks[12], (L, NSA_WIDTH + GMLP_WIDTH, D_MODEL), (NSA_WIDTH + GMLP_WIDTH) ** -0.5),
        "norm2_g": 1.0 + nrm(ks[13], (L, D_MODEL), 0.02),
        "w_up": nrm(ks[14], (L, D_MODEL, 2 * D_FF), D_MODEL ** -0.5),
        "conv_w": nrm(ks[15], (L, CONV_WIDTH, 2 * D_FF), CONV_WIDTH ** -0.5),
        "conv_b": nrm(ks[16], (L, 2 * D_FF), 0.02),
        "w_down": nrm(ks[17], (L, D_FF, D_MODEL), D_FF ** -0.5),
        "final_g": 1.0 + nrm(ks[18], (D_MODEL,), 0.02),
    }


def reference(x, positions, norm1_g, w_in, cmp_pe_k, cmp_w_k, cmp_pe_v, cmp_w_v,
              gmlp_norm_g, gmlp_w_s, gmlp_b_s, nsa_out_g, gmlp_out_g, w_out,
              norm2_g, w_up, conv_w, conv_b, w_down, final_g):
    B, S, _ = x.shape
    h = x
    for l in range(DEPTH):
        xn = rms_norm(h, norm1_g[l])
        proj = xn @ w_in[l]
        q = proj[..., :Q_END].reshape(B, S, NSA_HEADS, HEAD_DIM)
        kv = proj[..., Q_END:KV_END].reshape(B, S, 6, NSA_KV_GROUPS, HEAD_DIM)
        gates = proj[..., KV_END:GATE_END].reshape(B, S, NSA_HEADS, N_BRANCH)
        u = proj[..., GATE_END:U_END]
        v = proj[..., U_END:]
        o_nsa = nsa_mixer(q, kv[:, :, 0], kv[:, :, 1], kv[:, :, 2], kv[:, :, 3],
                          kv[:, :, 4], kv[:, :, 5], gates, positions,
                          cmp_pe_k[l], cmp_w_k[l], cmp_pe_v[l], cmp_w_v[l])
        o_gmlp = gmlp_mixer(u, v, gmlp_norm_g[l], gmlp_w_s[l], gmlp_b_s[l])
        mix = jnp.concatenate([rms_norm(o_nsa, nsa_out_g[l]), rms_norm(o_gmlp, gmlp_out_g[l])], axis=-1)
        h = h + mix @ w_out[l]
        h = h + conv_glu_ffn(rms_norm(h, norm2_g[l]), w_up[l], conv_w[l], conv_b[l], w_down[l])
    return rms_norm(h, final_g)
```

```python
import functools

import numpy as np
import jax
import jax.numpy as jnp
from jax import lax
from jax.experimental import pallas as pl
from jax.experimental.pallas import tpu as pltpu

F32 = jnp.float32
BF16 = jnp.bfloat16

HEAD_DIM = 128
NSA_HEADS = 8
NSA_KV_GROUPS = 2
NSA_REP = NSA_HEADS // NSA_KV_GROUPS
N_BRANCH = 3
CMP_LEN = 32
CMP_STRIDE = 16
SEL_LEN = 64
SEL_TOPK = 16
WINDOW = 512
GMLP_GROUP_DIM = 128
GMLP_GROUPS = 8
GMLP_CHUNK = 128
CONV_WIDTH = 3
ROPE_THETA = 10000.0
EPS = 1e-6
Q_BLOCK = 128
NEG_INF = -1e30
N_FORCED = 3
TOPK_FIRST_QB = SEL_TOPK * SEL_LEN // Q_BLOCK

V7X_LANES = 128
V7X_BF16_SUBLANE_PACK = 16
V7X_VMEM_LIMIT_BYTES = 56 * 1024 * 1024

TM_PROJ = 512
TM_MIX = 512
TM_FFN = 512
TF_FFN = 512
KV_CHUNK = 512
GATE_ROWS = 16


def _dot(a, b):
    return jnp.dot(a, b, preferred_element_type=F32)


def _dot_nt(a, b):
    return lax.dot_general(a, b, (((1,), (1,)), ((), ())), preferred_element_type=F32)


def _const_spec(shape):
    nd = len(shape)
    return pl.BlockSpec(shape, lambda *_: (0,) * nd, pipeline_mode=pl.Buffered(1))


def _params(semantics):
    return pltpu.CompilerParams(dimension_semantics=semantics,
                                vmem_limit_bytes=V7X_VMEM_LIMIT_BYTES)


def _gelu(x):
    return 0.5 * x * (1.0 + lax.erf(x * (2.0 ** -0.5)))


def _rope(x, cos, sin_signed):
    return x * cos + pltpu.roll(x, HEAD_DIM // 2, 1) * sin_signed


def _inproj_kernel(x_ref, pos_ref, g1_ref, inv_ref, sign_ref, wn_ref, wt_ref, gg_ref,
                   q_ref, ksw_ref, kvc_ref, u_ref, vg_ref, vt_ref, gt_ref):
    x = x_ref[...]
    ms = jnp.mean(x * x, axis=-1, keepdims=True)
    xn = (x * lax.rsqrt(ms + EPS) * g1_ref[...]).astype(BF16)

    ang = pos_ref[...].astype(F32) * inv_ref[...]
    cos = jnp.cos(ang)
    sin_s = jnp.sin(ang) * sign_ref[...]
    scale = HEAD_DIM ** -0.5

    seg_w = 4 * HEAD_DIM

    def seg(i):
        return _dot(xn, wn_ref[:, i * seg_w:(i + 1) * seg_w])

    for i in range(2):
        acc = seg(i)
        for h in range(4):
            qh = _rope(acc[:, h * HEAD_DIM:(h + 1) * HEAD_DIM], cos, sin_s) * scale
            c0 = (i * 4 + h) * HEAD_DIM
            q_ref[:, c0:c0 + HEAD_DIM] = qh.astype(BF16)
    acc = seg(2)
    for h in range(4):
        kh = _rope(acc[:, h * HEAD_DIM:(h + 1) * HEAD_DIM], cos, sin_s)
        ksw_ref[:, h * HEAD_DIM:(h + 1) * HEAD_DIM] = kh.astype(BF16)
    acc = seg(3)
    for s in range(4):
        kvc_ref[s] = acc[:, s * HEAD_DIM:(s + 1) * HEAD_DIM].astype(BF16)
    for i in range(2):
        acc = seg(4 + i)
        u_ref[:, i * seg_w:(i + 1) * seg_w] = _gelu(acc).astype(BF16)
    for i in range(2):
        acc = _gelu(seg(6 + i))
        for h in range(4):
            c0 = i * seg_w + h * GMLP_GROUP_DIM
            vh = acc[:, h * GMLP_GROUP_DIM:(h + 1) * GMLP_GROUP_DIM]
            msv = jnp.mean(vh * vh, axis=-1, keepdims=True)
            vn = vh * lax.rsqrt(msv + EPS) * gg_ref[:, c0:c0 + GMLP_GROUP_DIM]
            vg_ref[:, c0:c0 + GMLP_GROUP_DIM] = vn.astype(BF16)
    rt = _dot_nt(wt_ref[...], xn)
    nv = 4 * HEAD_DIM
    vt_ref[...] = rt[:nv].astype(BF16)
    gt_ref[...] = rt[nv:]


def _inproj(x2, pos2, g1, inv_full, sign, wn, wt, gg):
    T, D = x2.shape
    tm = TM_PROJ
    nt_rows = wt.shape[0]
    grid = (T // tm,)
    tok = lambda w: pl.BlockSpec((tm, w), lambda i: (i, 0))
    out_shape = (
        jax.ShapeDtypeStruct((T, NSA_HEADS * HEAD_DIM), BF16),
        jax.ShapeDtypeStruct((T, 4 * HEAD_DIM), BF16),
        jax.ShapeDtypeStruct((4, T, HEAD_DIM), BF16),
        jax.ShapeDtypeStruct((T, GMLP_GROUPS * GMLP_GROUP_DIM), BF16),
        jax.ShapeDtypeStruct((T, GMLP_GROUPS * GMLP_GROUP_DIM), BF16),
        jax.ShapeDtypeStruct((4 * HEAD_DIM, T), BF16),
        jax.ShapeDtypeStruct((NSA_KV_GROUPS * GATE_ROWS, T), F32),
    )
    out_specs = (
        tok(NSA_HEADS * HEAD_DIM),
        tok(4 * HEAD_DIM),
        pl.BlockSpec((4, tm, HEAD_DIM), lambda i: (0, i, 0)),
        tok(GMLP_GROUPS * GMLP_GROUP_DIM),
        tok(GMLP_GROUPS * GMLP_GROUP_DIM),
        pl.BlockSpec((4 * HEAD_DIM, tm), lambda i: (0, i)),
        pl.BlockSpec((NSA_KV_GROUPS * GATE_ROWS, tm), lambda i: (0, i)),
    )
    in_specs = [
        tok(D),
        pl.BlockSpec((tm, 1), lambda i: (i, 0)),
        _const_spec((1, D)),
        _const_spec((1, HEAD_DIM)),
        _const_spec((1, HEAD_DIM)),
        _const_spec(wn.shape),
        _const_spec((nt_rows, D)),
        _const_spec((1, GMLP_GROUPS * GMLP_GROUP_DIM)),
    ]
    return pl.pallas_call(
        _inproj_kernel, grid=grid, in_specs=in_specs, out_specs=out_specs,
        out_shape=out_shape, compiler_params=_params(("arbitrary",)),
        name="inproj",
    )(x2, pos2, g1, inv_full, sign, wn, wt, gg)


def _compress_body(x_ref, w2_ref, pe_ref, wflat_ref, shift_ref):
    ncp = x_ref.shape[0]
    p = _dot(x_ref[...], w2_ref[...])
    bias = _dot(pe_ref[...], wflat_ref[...])[0:1]
    shift_ref[0:ncp] = p[:, HEAD_DIM:]
    shift_ref[ncp:ncp + 8] = jnp.zeros((8, HEAD_DIM), F32)
    return p[:, :HEAD_DIM] + shift_ref[pl.ds(1, ncp), :] + bias


def _compress_k_kernel(x_ref, w2_ref, pe_ref, wflat_ref, posc_ref, inv_ref, sign_ref,
                       kc_ref, shift_ref):
    kc = _compress_body(x_ref, w2_ref, pe_ref, wflat_ref, shift_ref)
    ang = posc_ref[...].astype(F32) * inv_ref[...]
    kc_ref[...] = _rope(kc, jnp.cos(ang), jnp.sin(ang) * sign_ref[...]).astype(BF16)


def _compress_v_kernel(x_ref, w2_ref, pe_ref, wflat_ref, vct_ref, shift_ref):
    vc = _compress_body(x_ref, w2_ref, pe_ref, wflat_ref, shift_ref)
    vct_ref[...] = vc.T.astype(BF16)


def _compress(kvc, w2, pe8, wflat, batch, rope_args=None):
    ncp = kvc.shape[1] // batch
    kdim = kvc.shape[2]
    grid = (batch, NSA_KV_GROUPS)
    plane0 = 0 if rope_args is not None else NSA_KV_GROUPS
    in_specs = [
        pl.BlockSpec((None, ncp, kdim), lambda b, g: (plane0 + g, b, 0)),
        _const_spec(w2.shape),
        _const_spec(pe8.shape),
        _const_spec(wflat.shape),
    ]
    scratch = [pltpu.VMEM((ncp + 8, HEAD_DIM), F32)]
    if rope_args is not None:
        posc, inv_full, sign = rope_args
        in_specs += [pl.BlockSpec((ncp, 1), lambda b, g: (b, 0)),
                     _const_spec((1, HEAD_DIM)), _const_spec((1, HEAD_DIM))]
        return pl.pallas_call(
            _compress_k_kernel, grid=grid, in_specs=in_specs,
            out_specs=pl.BlockSpec((None, ncp, HEAD_DIM), lambda b, g: (g, b, 0)),
            out_shape=jax.ShapeDtypeStruct((NSA_KV_GROUPS, batch * ncp, HEAD_DIM), BF16),
            scratch_shapes=scratch, compiler_params=_params(("arbitrary", "arbitrary")),
            name="compress_k",
        )(kvc, w2, pe8, wflat, posc, inv_full, sign)
    return pl.pallas_call(
        _compress_v_kernel, grid=grid, in_specs=in_specs,
        out_specs=pl.BlockSpec((None, HEAD_DIM, ncp), lambda b, g: (g, 0, b)),
        out_shape=jax.ShapeDtypeStruct((NSA_KV_GROUPS, HEAD_DIM, batch * ncp), BF16),
        scratch_shapes=scratch, compiler_params=_params(("arbitrary", "arbitrary")),
        name="compress_v",
    )(kvc, w2, pe8, wflat)


def _nsa_kernel(q_ref, kc_ref, vct_ref, ksel_ref, kwin_ref, vselt_ref, vwint_ref, gt_ref,
                aggt_ref, o_ref, selb_ref):
    qb = pl.program_id(2)
    nq = NSA_REP * Q_BLOCK
    ncp = kc_ref.shape[0]
    nsel = aggt_ref.shape[0]

    q4 = jnp.concatenate([q_ref[:, r * HEAD_DIM:(r + 1) * HEAD_DIM] for r in range(NSA_REP)],
                         axis=0)
    lane4 = lax.broadcasted_iota(jnp.int32, (1, nq), 1)
    t4 = qb * Q_BLOCK + (lane4 & (Q_BLOCK - 1))

    s = _dot_nt(kc_ref[...], q4)
    n_idx = lax.broadcasted_iota(jnp.int32, (ncp, 1), 0)
    cmask = (n_idx * CMP_STRIDE + (CMP_LEN - 1)) <= t4
    s = jnp.where(cmask, s, NEG_INF)
    m = jnp.max(s, axis=0, keepdims=True)
    e = jnp.where(cmask, jnp.exp(s - m), 0.0)
    l = jnp.sum(e, axis=0, keepdims=True)
    p = e * jnp.where(l > 0.0, 1.0 / l, 0.0)
    o_cmp = _dot(vct_ref[...], p.astype(BF16))

    ps = p[:, 0:Q_BLOCK]
    for r in range(1, NSA_REP):
        ps = ps + p[:, r * Q_BLOCK:(r + 1) * Q_BLOCK]
    ps_hi = ps.astype(BF16)
    ps_lo = (ps - ps_hi.astype(F32)).astype(BF16)
    aggt = aggt_ref[...]
    imp = _dot(aggt, ps_hi) + _dot(aggt, ps_lo)

    jj = lax.broadcasted_iota(jnp.int32, (nsel, Q_BLOCK), 0)
    t1 = qb * Q_BLOCK + lax.broadcasted_iota(jnp.int32, (1, Q_BLOCK), 1)
    cur = lax.shift_right_logical(t1, SEL_LEN.bit_length() - 1)
    valid = jj <= cur

    @pl.when(qb < TOPK_FIRST_QB)
    def _():
        selb_ref[...] = jnp.where(valid, 0.0, NEG_INF)

    @pl.when(qb >= TOPK_FIRST_QB)
    def _():
        forced = (jj == 0) | (jj == cur) | (jj == cur - 1)
        lowest = -3.0e38
        w0 = jnp.where(valid & jnp.logical_not(forced), imp, lowest)
        sel0 = jnp.where(forced, 1.0, 0.0)

        def pick(_, carry):
            w, sel = carry
            mx = jnp.max(w, axis=0, keepdims=True)
            idx = jnp.min(jnp.where(w == mx, jj, nsel), axis=0, keepdims=True)
            chosen = jj == idx
            return jnp.where(chosen, lowest, w), jnp.where(chosen, 1.0, sel)

        _, sel = lax.fori_loop(0, SEL_TOPK - N_FORCED, pick, (w0, sel0))
        selb_ref[...] = jnp.where(sel > 0.5, 0.0, NEG_INF)

    blocks_per_chunk = KV_CHUNK // SEL_LEN
    k_idx = lax.broadcasted_iota(jnp.int32, (KV_CHUNK, 1), 0)

    def sel_step(c, carry, causal):
        m_i, l_i, acc = carry
        k0 = pl.multiple_of(c * KV_CHUNK, KV_CHUNK)
        sc = _dot_nt(ksel_ref[pl.ds(k0, KV_CHUNK), :], q4)
        sb = selb_ref[pl.ds(pl.multiple_of(c * blocks_per_chunk, blocks_per_chunk),
                            blocks_per_chunk), :]
        bias = jnp.concatenate(
            [jnp.broadcast_to(sb[i:i + 1, :], (SEL_LEN, Q_BLOCK)) for i in range(blocks_per_chunk)],
            axis=0)
        sc = sc + jnp.concatenate([bias] * NSA_REP, axis=1)
        if causal:
            sc = jnp.where((k0 + k_idx) <= t4, sc, NEG_INF)
        m_new = jnp.maximum(m_i, jnp.max(sc, axis=0, keepdims=True))
        alpha = jnp.exp(m_i - m_new)
        pc = jnp.exp(sc - m_new)
        l_new = alpha * l_i + jnp.sum(pc, axis=0, keepdims=True)
        pv = _dot(vselt_ref[:, pl.ds(k0, KV_CHUNK)], pc.astype(BF16))
        return m_new, l_new, alpha * acc + pv

    n_chunks = qb // (KV_CHUNK // Q_BLOCK) + 1
    init = (jnp.full((1, nq), NEG_INF, F32), jnp.zeros((1, nq), F32),
            jnp.zeros((HEAD_DIM, nq), F32))
    carry = lax.fori_loop(0, n_chunks - 1, functools.partial(sel_step, causal=False), init)
    _, l_sel, acc_sel = sel_step(n_chunks - 1, carry, causal=True)
    o_sel = acc_sel * (1.0 / l_sel)

    wk = WINDOW + Q_BLOCK
    w0_ = pl.multiple_of(jnp.maximum(qb * Q_BLOCK - WINDOW, 0), Q_BLOCK)
    sw = _dot_nt(kwin_ref[pl.ds(w0_, wk), :], q4)
    diff = t4 - (w0_ + lax.broadcasted_iota(jnp.int32, (wk, 1), 0))
    wmask = (diff >= 0) & (diff < WINDOW)
    sw = jnp.where(wmask, sw, NEG_INF)
    mw = jnp.max(sw, axis=0, keepdims=True)
    ew = jnp.where(wmask, jnp.exp(sw - mw), 0.0)
    lw = jnp.sum(ew, axis=0, keepdims=True)
    o_win = _dot(vwint_ref[:, pl.ds(w0_, wk)], ew.astype(BF16)) * (1.0 / lw)

    gate = jax.nn.sigmoid(gt_ref[...])
    for r in range(NSA_REP):
        cs = slice(r * Q_BLOCK, (r + 1) * Q_BLOCK)
        g0 = gate[N_BRANCH * r + 0:N_BRANCH * r + 1, :]
        g1 = gate[N_BRANCH * r + 1:N_BRANCH * r + 2, :]
        g2 = gate[N_BRANCH * r + 2:N_BRANCH * r + 3, :]
        ot = g0 * o_cmp[:, cs] + g1 * o_sel[:, cs] + g2 * o_win[:, cs]
        o_ref[:, r * HEAD_DIM:(r + 1) * HEAD_DIM] = ot.T


def _nsa(q, kc, vct, ksw, vt, gt, aggt, batch, seq):
    T = q.shape[0]
    n_qb = seq // Q_BLOCK
    ncp = kc.shape[1] // batch
    nsel = aggt.shape[0]
    gw = NSA_REP * HEAD_DIM
    big = lambda shape, imap: pl.BlockSpec(shape, imap, pipeline_mode=pl.Buffered(1))
    in_specs = [
        pl.BlockSpec((Q_BLOCK, gw), lambda b, g, i: (b * n_qb + i, g)),
        big((None, ncp, HEAD_DIM), lambda b, g, i: (g, b, 0)),
        big((None, HEAD_DIM, ncp), lambda b, g, i: (g, 0, b)),
        big((seq, HEAD_DIM), lambda b, g, i: (b, g)),
        big((seq, HEAD_DIM), lambda b, g, i: (b, NSA_KV_GROUPS + g)),
        big((HEAD_DIM, seq), lambda b, g, i: (g, b)),
        big((HEAD_DIM, seq), lambda b, g, i: (NSA_KV_GROUPS + g, b)),
        pl.BlockSpec((GATE_ROWS, Q_BLOCK), lambda b, g, i: (g, b * n_qb + i)),
        _const_spec(aggt.shape),
    ]
    return pl.pallas_call(
        _nsa_kernel, grid=(batch, NSA_KV_GROUPS, n_qb), in_specs=in_specs,
        out_specs=pl.BlockSpec((Q_BLOCK, gw), lambda b, g, i: (b * n_qb + i, g)),
        out_shape=jax.ShapeDtypeStruct((T, NSA_HEADS * HEAD_DIM), F32),
        scratch_shapes=[pltpu.VMEM((nsel, Q_BLOCK), F32)],
        compiler_params=_params(("arbitrary", "arbitrary", "arbitrary")),
        name="nsa",
    )(q, kc, vct, ksw, ksw, vt, vt, gt, aggt)


def _mix_out_kernel(x_ref, on_ref, u_ref, vg_ref, ws_ref, bst_ref, gn_ref, gm_ref, wo_ref,
                    g2_ref, h_ref, y_ref, og_ref):
    tm = x_ref.shape[0]
    row = lax.broadcasted_iota(jnp.int32, (GMLP_CHUNK, GMLP_CHUNK), 0)
    col = lax.broadcasted_iota(jnp.int32, (GMLP_CHUNK, GMLP_CHUNK), 1)
    tril = col <= row
    for h in range(GMLP_GROUPS):
        ws = jnp.where(tril, ws_ref[h], 0.0).astype(BF16)
        bcol = bst_ref[:, h:h + 1]
        cs = slice(h * GMLP_GROUP_DIM, (h + 1) * GMLP_GROUP_DIM)
        for n in range(tm // GMLP_CHUNK):
            rs = slice(n * GMLP_CHUNK, (n + 1) * GMLP_CHUNK)
            mixed = _dot(ws, vg_ref[rs, cs]) + bcol
            og_ref[rs, cs] = u_ref[rs, cs].astype(F32) * mixed

    def rms(v, g):
        return v * lax.rsqrt(jnp.mean(v * v, axis=-1, keepdims=True) + EPS) * g

    half = on_ref.shape[1]
    mix_n = rms(on_ref[...], gn_ref[...]).astype(BF16)
    mix_g = rms(og_ref[...], gm_ref[...]).astype(BF16)
    h1 = x_ref[...] + _dot(mix_n, wo_ref[0:half, :]) + _dot(mix_g, wo_ref[half:, :])
    h_ref[...] = h1
    y_ref[...] = rms(h1, g2_ref[...]).astype(BF16)


def _mix_out(x2, o_nsa, u, vg, ws, bst, gn, gm, wo, g2):
    T, D = x2.shape
    tm = TM_MIX
    half = o_nsa.shape[1]
    tok = lambda w: pl.BlockSpec((tm, w), lambda i: (i, 0))
    in_specs = [tok(D), tok(half), tok(half), tok(half),
                _const_spec(ws.shape), _const_spec(bst.shape),
                _const_spec((1, half)), _const_spec((1, half)),
                _const_spec(wo.shape), _const_spec((1, D))]
    return pl.pallas_call(
        _mix_out_kernel, grid=(T // tm,), in_specs=in_specs,
        out_specs=(tok(D), tok(D)),
        out_shape=(jax.ShapeDtypeStruct((T, D), F32), jax.ShapeDtypeStruct((T, D), BF16)),
        scratch_shapes=[pltpu.VMEM((tm, half), F32)],
        compiler_params=_params(("arbitrary",)),
        name="mix_out",
    )(x2, o_nsa, u, vg, ws, bst, gn, gm, wo, g2)


def _ffn_kernel(y_ref, halo_ref, wg_ref, wu_ref, cwg_ref, cwu_ref, cbg_ref, cbu_ref, wd_ref,
                h_ref, gf_ref, o_ref, ybuf_ref, a_ref, acc_ref, *, tiles_per_seq):
    i = pl.program_id(0)
    j = pl.program_id(1)
    tm = y_ref.shape[0]
    tf = wg_ref.shape[1]
    hr = halo_ref.shape[0]

    @pl.when(j == 0)
    def _():
        keep = jnp.where(i % tiles_per_seq == 0, 0.0, 1.0).astype(BF16)
        ybuf_ref[0:hr] = halo_ref[...] * keep
        ybuf_ref[hr:] = y_ref[...]
        acc_ref[...] = h_ref[...]

    yb = ybuf_ref[...]
    a_ref[:, 0:tf] = _dot(yb, wg_ref[...])
    a_ref[:, tf:] = _dot(yb, wu_ref[...])

    def conv(cols, cw_ref, cb_ref):
        c = cb_ref[...] + cw_ref[CONV_WIDTH - 1:CONV_WIDTH, :] * a_ref[pl.ds(hr, tm), cols]
        for k in range(CONV_WIDTH - 1):
            shift = CONV_WIDTH - 1 - k
            c = c + cw_ref[k:k + 1, :] * a_ref[pl.ds(hr - shift, tm), cols]
        return c

    cg = conv(slice(0, tf), cwg_ref, cbg_ref)
    cu = conv(slice(tf, 2 * tf), cwu_ref, cbu_ref)
    hmid = (cg * jax.nn.sigmoid(cg) * cu).astype(BF16)
    acc_ref[...] += _dot(hmid, wd_ref[...])

    @pl.when(j == pl.num_programs(1) - 1)
    def _():
        hh = acc_ref[...]
        ms = jnp.mean(hh * hh, axis=-1, keepdims=True)
        o_ref[...] = hh * lax.rsqrt(ms + EPS) * gf_ref[...]


def _ffn(y, h1, w_up, conv_w, conv_b, w_down, gf, seq):
    T, D = h1.shape
    dff = w_down.shape[0]
    tm, tf = TM_FFN, TF_FFN
    hr = V7X_BF16_SUBLANE_PACK
    nj = dff // tf
    halo_blocks = tm // hr
    in_specs = [
        pl.BlockSpec((tm, D), lambda i, j: (i, 0)),
        pl.BlockSpec((hr, D), lambda i, j: (jnp.maximum(i * halo_blocks - 1, 0), 0)),
        pl.BlockSpec((D, tf), lambda i, j: (0, j)),
        pl.BlockSpec((D, tf), lambda i, j: (0, nj + j)),
        pl.BlockSpec((CONV_WIDTH, tf), lambda i, j: (0, j)),
        pl.BlockSpec((CONV_WIDTH, tf), lambda i, j: (0, nj + j)),
        pl.BlockSpec((1, tf), lambda i, j: (0, j)),
        pl.BlockSpec((1, tf), lambda i, j: (0, nj + j)),
        pl.BlockSpec((tf, D), lambda i, j: (j, 0)),
        pl.BlockSpec((tm, D), lambda i, j: (i, 0)),
        _const_spec((1, D)),
    ]
    return pl.pallas_call(
        functools.partial(_ffn_kernel, tiles_per_seq=seq // tm),
        grid=(T // tm, nj), in_specs=in_specs,
        out_specs=pl.BlockSpec((tm, D), lambda i, j: (i, 0)),
        out_shape=jax.ShapeDtypeStruct((T, D), F32),
        scratch_shapes=[pltpu.VMEM((tm + hr, D), BF16),
                        pltpu.VMEM((tm + hr, 2 * tf), F32),
                        pltpu.VMEM((tm, D), F32)],
        compiler_params=_params(("arbitrary", "arbitrary")),
        name="ffn",
    )(y, y, w_up, w_up, conv_w, conv_w, conv_b, conv_b, w_down, h1, gf)


def _agg_t(n_sel, ncp, n_cmp):
    c_start = np.arange(ncp) * CMP_STRIDE
    js = np.arange(n_sel)[:, None] * SEL_LEN
    a = (c_start[None, :] < js + SEL_LEN) & (c_start[None, :] + CMP_LEN > js)
    a &= (np.arange(ncp) < n_cmp)[None, :]
    return jnp.asarray(a.astype(np.float32), dtype=BF16)


def _layer(h, positions, norm1_g, w_in, cmp_pe_k, cmp_w_k, cmp_pe_v, cmp_w_v, gmlp_norm_g,
           gmlp_w_s, gmlp_b_s, nsa_out_g, gmlp_out_g, w_out, norm2_g, w_up, conv_w, conv_b,
           w_down, out_g):
    B, S, D = h.shape
    T = B * S
    nq = NSA_HEADS * HEAD_DIM
    kvw = NSA_KV_GROUPS * HEAD_DIM
    x2 = h.reshape(T, D)
    pos2 = positions.reshape(T, 1).astype(jnp.int32)

    half = HEAD_DIM // 2
    inv = ROPE_THETA ** (-2.0 * np.arange(half, dtype=np.float32) / HEAD_DIM)
    inv_full = jnp.asarray(np.concatenate([inv, inv])[None, :], dtype=F32)
    sign = jnp.asarray(np.concatenate([-np.ones(half), np.ones(half)])[None, :], dtype=F32)

    kv_w = w_in[:, nq:nq + 6 * kvw].reshape(D, 6, kvw)
    k_cmp_w, v_cmp_w, k_sel_w, v_sel_w, k_win_w, v_win_w = (kv_w[:, i] for i in range(6))
    g_off = nq + 6 * kvw
    n_gate = NSA_HEADS * N_BRANCH
    gate_w = w_in[:, g_off:g_off + n_gate].reshape(D, NSA_KV_GROUPS, NSA_REP * N_BRANCH)
    gate_w = jnp.pad(gate_w, ((0, 0), (0, 0), (0, GATE_ROWS - NSA_REP * N_BRANCH)))
    gate_w = gate_w.reshape(D, NSA_KV_GROUPS * GATE_ROWS)
    u_off = g_off + n_gate
    gw = GMLP_GROUPS * GMLP_GROUP_DIM
    u_w = w_in[:, u_off:u_off + gw]
    v_w = w_in[:, u_off + gw:u_off + 2 * gw]
    wn = jnp.concatenate([w_in[:, :nq], k_sel_w, k_win_w, k_cmp_w, v_cmp_w, u_w, v_w],
                         axis=1).astype(BF16)
    wt = jnp.concatenate([v_sel_w, v_win_w, gate_w], axis=1).T.astype(BF16)

    q, ksw, kvc, u, vg, vt, gt = _inproj(
        x2, pos2, norm1_g.reshape(1, D), inv_full, sign, wn, wt, gmlp_norm_g.reshape(1, gw))

    ncp = S // CMP_STRIDE
    n_cmp = (S - CMP_LEN) // CMP_STRIDE + 1
    hb = CMP_LEN // 2
    kvc2 = kvc.reshape(4, B * ncp, CMP_STRIDE * HEAD_DIM)

    def cmp_weights(w, pe):
        w2 = jnp.concatenate([w[:hb].reshape(hb * HEAD_DIM, HEAD_DIM),
                              w[hb:].reshape(hb * HEAD_DIM, HEAD_DIM)], axis=1).astype(BF16)
        pe8 = jnp.pad(pe.reshape(1, CMP_LEN * HEAD_DIM), ((0, 7), (0, 0))).astype(BF16)
        return w2, pe8, w.reshape(CMP_LEN * HEAD_DIM, HEAD_DIM).astype(BF16)

    posc = jnp.pad(positions[:, CMP_LEN - 1::CMP_STRIDE], ((0, 0), (0, ncp - n_cmp)))
    posc = posc.reshape(B * ncp, 1).astype(jnp.int32)
    kc = _compress(kvc2, *cmp_weights(cmp_w_k, cmp_pe_k), B, rope_args=(posc, inv_full, sign))
    vct = _compress(kvc2, *cmp_weights(cmp_w_v, cmp_pe_v), B)

    o_nsa = _nsa(q, kc, vct, ksw, vt, gt, _agg_t(S // SEL_LEN, ncp, n_cmp), B, S)

    h1, y = _mix_out(x2, o_nsa, u, vg, gmlp_w_s, gmlp_b_s.T, nsa_out_g.reshape(1, nq),
                     gmlp_out_g.reshape(1, gw), w_out.astype(BF16), norm2_g.reshape(1, D))

    out = _ffn(y, h1, w_up.astype(BF16), conv_w, conv_b.reshape(1, -1), w_down.astype(BF16),
               out_g.reshape(1, D), S)
    return out.reshape(B, S, D)


def kernel(x, positions, norm1_g, w_in, cmp_pe_k, cmp_w_k, cmp_pe_v, cmp_w_v, gmlp_norm_g,
           gmlp_w_s, gmlp_b_s, nsa_out_g, gmlp_out_g, w_out, norm2_g, w_up, conv_w, conv_b,
           w_down, final_g):
    depth = norm1_g.shape[0]
    assert depth == 1, "the FFN kernel fuses the final RMSNorm into the only layer"
    return _layer(x, positions, norm1_g[0], w_in[0], cmp_pe_k[0], cmp_w_k[0], cmp_pe_v[0],
                  cmp_w_v[0], gmlp_norm_g[0], gmlp_w_s[0], gmlp_b_s[0], nsa_out_g[0],
                  gmlp_out_g[0], w_out[0], norm2_g[0], w_up[0], conv_w[0], conv_b[0],
                  w_down[0], final_g)
```

```python
import functools

import numpy as np
import jax
import jax.numpy as jnp
from jax import lax
from jax.experimental import pallas as pl
from jax.experimental.pallas import tpu as pltpu

F32 = jnp.float32
BF16 = jnp.bfloat16

HEAD_DIM = 128
NSA_HEADS = 8
NSA_KV_GROUPS = 2
NSA_REP = NSA_HEADS // NSA_KV_GROUPS
N_BRANCH = 3
CMP_LEN = 32
CMP_STRIDE = 16
SEL_LEN = 64
SEL_TOPK = 16
WINDOW = 512
GMLP_GROUP_DIM = 128
GMLP_GROUPS = 8
GMLP_CHUNK = 128
CONV_WIDTH = 3
ROPE_THETA = 10000.0
EPS = 1e-6
Q_BLOCK = 128
NEG_INF = -1e30
N_FORCED = 3
TOPK_FIRST_QB = SEL_TOPK * SEL_LEN // Q_BLOCK

V7X_LANES = 128
V7X_BF16_SUBLANE_PACK = 16
V7X_VMEM_LIMIT_BYTES = 56 * 1024 * 1024

TM_PROJ = 512
TM_MIX = 512
TM_FFN = 512
TF_FFN = 512
KV_CHUNK = 512
SEL_UNROLL = 2
BIAS_SLOTS = 128
GATE_ROWS = 16


def _dot(a, b):
    return jnp.dot(a, b, preferred_element_type=F32)


def _dot_nt(a, b):
    return lax.dot_general(a, b, (((1,), (1,)), ((), ())), preferred_element_type=F32)


def _const_spec(shape):
    nd = len(shape)
    return pl.BlockSpec(shape, lambda *_: (0,) * nd, pipeline_mode=pl.Buffered(1))


def _params(semantics):
    return pltpu.CompilerParams(dimension_semantics=semantics,
                                vmem_limit_bytes=V7X_VMEM_LIMIT_BYTES)


def _gelu(x):
    return 0.5 * x * (1.0 + lax.erf(x * (2.0 ** -0.5)))


def _rope(x, cos, sin_signed):
    return x * cos + pltpu.roll(x, HEAD_DIM // 2, 1) * sin_signed


def _inproj_kernel(x_ref, pos_ref, g1_ref, inv_ref, sign_ref, wn_ref, wt_ref, gg_ref,
                   q_ref, ksw_ref, kvc_ref, u_ref, vg_ref, vt_ref, gt_ref):
    x = x_ref[...]
    ms = jnp.mean(x * x, axis=-1, keepdims=True)
    xn = (x * lax.rsqrt(ms + EPS) * g1_ref[...]).astype(BF16)

    ang = pos_ref[...].astype(F32) * inv_ref[...]
    cos = jnp.cos(ang)
    sin_s = jnp.sin(ang) * sign_ref[...]
    scale = HEAD_DIM ** -0.5

    seg_w = 4 * HEAD_DIM

    def seg(i):
        return _dot(xn, wn_ref[:, i * seg_w:(i + 1) * seg_w])

    for i in range(2):
        acc = seg(i)
        for h in range(4):
            qh = _rope(acc[:, h * HEAD_DIM:(h + 1) * HEAD_DIM], cos, sin_s) * scale
            c0 = (i * 4 + h) * HEAD_DIM
            q_ref[:, c0:c0 + HEAD_DIM] = qh.astype(BF16)
    acc = seg(2)
    for h in range(4):
        kh = _rope(acc[:, h * HEAD_DIM:(h + 1) * HEAD_DIM], cos, sin_s)
        ksw_ref[:, h * HEAD_DIM:(h + 1) * HEAD_DIM] = kh.astype(BF16)
    acc = seg(3)
    for s in range(4):
        kvc_ref[s] = acc[:, s * HEAD_DIM:(s + 1) * HEAD_DIM].astype(BF16)
    for i in range(2):
        acc = seg(4 + i)
        u_ref[:, i * seg_w:(i + 1) * seg_w] = _gelu(acc).astype(BF16)
    for i in range(2):
        acc = _gelu(seg(6 + i))
        for h in range(4):
            c0 = i * seg_w + h * GMLP_GROUP_DIM
            vh = acc[:, h * GMLP_GROUP_DIM:(h + 1) * GMLP_GROUP_DIM]
            msv = jnp.mean(vh * vh, axis=-1, keepdims=True)
            vn = vh * lax.rsqrt(msv + EPS) * gg_ref[:, c0:c0 + GMLP_GROUP_DIM]
            vg_ref[:, c0:c0 + GMLP_GROUP_DIM] = vn.astype(BF16)
    rt = _dot_nt(wt_ref[...], xn)
    nv = 4 * HEAD_DIM
    vt_ref[...] = rt[:nv].astype(BF16)
    gt_ref[...] = rt[nv:]


def _inproj(x2, pos2, g1, inv_full, sign, wn, wt, gg):
    T, D = x2.shape
    tm = TM_PROJ
    nt_rows = wt.shape[0]
    grid = (T // tm,)
    tok = lambda w: pl.BlockSpec((tm, w), lambda i: (i, 0))
    out_shape = (
        jax.ShapeDtypeStruct((T, NSA_HEADS * HEAD_DIM), BF16),
        jax.ShapeDtypeStruct((T, 4 * HEAD_DIM), BF16),
        jax.ShapeDtypeStruct((4, T, HEAD_DIM), BF16),
        jax.ShapeDtypeStruct((T, GMLP_GROUPS * GMLP_GROUP_DIM), BF16),
        jax.ShapeDtypeStruct((T, GMLP_GROUPS * GMLP_GROUP_DIM), BF16),
        jax.ShapeDtypeStruct((4 * HEAD_DIM, T), BF16),
        jax.ShapeDtypeStruct((NSA_KV_GROUPS * GATE_ROWS, T), F32),
    )
    out_specs = (
        tok(NSA_HEADS * HEAD_DIM),
        tok(4 * HEAD_DIM),
        pl.BlockSpec((4, tm, HEAD_DIM), lambda i: (0, i, 0)),
        tok(GMLP_GROUPS * GMLP_GROUP_DIM),
        tok(GMLP_GROUPS * GMLP_GROUP_DIM),
        pl.BlockSpec((4 * HEAD_DIM, tm), lambda i: (0, i)),
        pl.BlockSpec((NSA_KV_GROUPS * GATE_ROWS, tm), lambda i: (0, i)),
    )
    in_specs = [
        tok(D),
        pl.BlockSpec((tm, 1), lambda i: (i, 0)),
        _const_spec((1, D)),
        _const_spec((1, HEAD_DIM)),
        _const_spec((1, HEAD_DIM)),
        _const_spec(wn.shape),
        _const_spec((nt_rows, D)),
        _const_spec((1, GMLP_GROUPS * GMLP_GROUP_DIM)),
    ]
    return pl.pallas_call(
        _inproj_kernel, grid=grid, in_specs=in_specs, out_specs=out_specs,
        out_shape=out_shape, compiler_params=_params(("arbitrary",)),
        name="inproj",
    )(x2, pos2, g1, inv_full, sign, wn, wt, gg)


def _compress_body(x_ref, w2_ref, pe_ref, wflat_ref, shift_ref):
    ncp = x_ref.shape[0]
    p = _dot(x_ref[...], w2_ref[...])
    bias = _dot(pe_ref[...], wflat_ref[...])[0:1]
    shift_ref[0:ncp] = p[:, HEAD_DIM:]
    shift_ref[ncp:ncp + 8] = jnp.zeros((8, HEAD_DIM), F32)
    return p[:, :HEAD_DIM] + shift_ref[pl.ds(1, ncp), :] + bias


def _compress_k_kernel(x_ref, w2_ref, pe_ref, wflat_ref, posc_ref, inv_ref, sign_ref,
                       kc_ref, shift_ref):
    kc = _compress_body(x_ref, w2_ref, pe_ref, wflat_ref, shift_ref)
    ang = posc_ref[...].astype(F32) * inv_ref[...]
    kc_ref[...] = _rope(kc, jnp.cos(ang), jnp.sin(ang) * sign_ref[...]).astype(BF16)


def _compress_v_kernel(x_ref, w2_ref, pe_ref, wflat_ref, vct_ref, shift_ref):
    vc = _compress_body(x_ref, w2_ref, pe_ref, wflat_ref, shift_ref)
    vct_ref[...] = vc.T.astype(BF16)


def _compress(kvc, w2, pe8, wflat, batch, rope_args=None):
    ncp = kvc.shape[1] // batch
    kdim = kvc.shape[2]
    grid = (batch, NSA_KV_GROUPS)
    plane0 = 0 if rope_args is not None else NSA_KV_GROUPS
    in_specs = [
        pl.BlockSpec((None, ncp, kdim), lambda b, g: (plane0 + g, b, 0)),
        _const_spec(w2.shape),
        _const_spec(pe8.shape),
        _const_spec(wflat.shape),
    ]
    scratch = [pltpu.VMEM((ncp + 8, HEAD_DIM), F32)]
    if rope_args is not None:
        posc, inv_full, sign = rope_args
        in_specs += [pl.BlockSpec((ncp, 1), lambda b, g: (b, 0)),
                     _const_spec((1, HEAD_DIM)), _const_spec((1, HEAD_DIM))]
        return pl.pallas_call(
            _compress_k_kernel, grid=grid, in_specs=in_specs,
            out_specs=pl.BlockSpec((None, ncp, HEAD_DIM), lambda b, g: (g, b, 0)),
            out_shape=jax.ShapeDtypeStruct((NSA_KV_GROUPS, batch * ncp, HEAD_DIM), BF16),
            scratch_shapes=scratch, compiler_params=_params(("arbitrary", "arbitrary")),
            name="compress_k",
        )(kvc, w2, pe8, wflat, posc, inv_full, sign)
    return pl.pallas_call(
        _compress_v_kernel, grid=grid, in_specs=in_specs,
        out_specs=pl.BlockSpec((None, HEAD_DIM, ncp), lambda b, g: (g, 0, b)),
        out_shape=jax.ShapeDtypeStruct((NSA_KV_GROUPS, HEAD_DIM, batch * ncp), BF16),
        scratch_shapes=scratch, compiler_params=_params(("arbitrary", "arbitrary")),
        name="compress_v",
    )(kvc, w2, pe8, wflat)


def _nsa_kernel(q_ref, kc_ref, vct_ref, ksel_ref, kwin_ref, vselt_ref, vwint_ref, gt_ref,
                aggt_ref, eblk_ref, o_ref, qaug_ref, sc_ref):
    qb = pl.program_id(2)
    nq = NSA_REP * Q_BLOCK
    ncp = kc_ref.shape[0]
    nsel = aggt_ref.shape[0]
    n_var = qaug_ref.shape[0]
    chunks_per_var = eblk_ref.shape[0]

    qt = jnp.concatenate(
        [q_ref[:, r * HEAD_DIM:(r + 1) * HEAD_DIM].astype(F32).T for r in range(NSA_REP)],
        axis=1).astype(BF16)
    lane4 = lax.broadcasted_iota(jnp.int32, (1, nq), 1)
    t4 = qb * Q_BLOCK + (lane4 & (Q_BLOCK - 1))

    s = _dot(kc_ref[...], qt)
    wk = WINDOW + Q_BLOCK
    w0_ = pl.multiple_of(jnp.maximum(qb * Q_BLOCK - WINDOW, 0), Q_BLOCK)
    sw = _dot(kwin_ref[pl.ds(w0_, wk), :], qt)

    n_idx = lax.broadcasted_iota(jnp.int32, (ncp, 1), 0)
    cmask = (n_idx * CMP_STRIDE + (CMP_LEN - 1)) <= t4
    s = jnp.where(cmask, s, NEG_INF)
    m = jnp.max(s, axis=0, keepdims=True)
    e = jnp.where(cmask, jnp.exp(s - m), 0.0)
    l = jnp.sum(e, axis=0, keepdims=True)
    p = e * jnp.where(l > 0.0, 1.0 / l, 0.0)
    o_cmp = _dot(vct_ref[...], p.astype(BF16))

    ps = p[:, 0:Q_BLOCK]
    for r in range(1, NSA_REP):
        ps = ps + p[:, r * Q_BLOCK:(r + 1) * Q_BLOCK]
    ps_hi = ps.astype(BF16)
    ps_lo = (ps - ps_hi.astype(F32)).astype(BF16)
    aggt = aggt_ref[...]
    imp = _dot(aggt, ps_hi) + _dot(aggt, ps_lo)

    diff = t4 - (w0_ + lax.broadcasted_iota(jnp.int32, (wk, 1), 0))
    wmask = (diff >= 0) & (diff < WINDOW)
    sw = jnp.where(wmask, sw, NEG_INF)
    mw = jnp.max(sw, axis=0, keepdims=True)
    ew = jnp.where(wmask, jnp.exp(sw - mw), 0.0)
    lw = jnp.sum(ew, axis=0, keepdims=True)
    o_win = _dot(vwint_ref[:, pl.ds(w0_, wk)], ew.astype(BF16)) * (1.0 / lw)

    jj =lax.broadcasted_iota(jnp.int32, (nsel, Q_BLOCK), 0)
    t1 = qb * Q_BLOCK + lax.broadcasted_iota(jnp.int32, (1, Q_BLOCK), 1)
    cur = t1 >> (SEL_LEN.bit_length() - 1)
    valid = jj <= cur

    forced = (jj == 0) | (jj == cur) | (jj == cur - 1)
    cand = valid & jnp.logical_not(forced)
    lowest = -1.0
    w = jnp.where(cand, imp, lowest)
    for _ in range(SEL_TOPK - N_FORCED):
        mx = jnp.max(w, axis=0, keepdims=True)
        idx = jnp.min(jnp.where(w == mx, jj, nsel), axis=0, keepdims=True)
        w = jnp.where(jj == idx, lowest, w)
    picked = forced | (cand & (w == lowest))
    selb = jnp.where(qb < TOPK_FIRST_QB,
                     jnp.where(valid, 0.0, NEG_INF), jnp.where(picked, 0.0, NEG_INF))

    selb4 = jnp.concatenate([selb] * NSA_REP, axis=1)
    pad_rows = n_var * BIAS_SLOTS - nsel
    if pad_rows:
        selb4 = jnp.concatenate([selb4, jnp.zeros((pad_rows, nq), F32)], axis=0)
    for v in range(n_var):
        qaug_ref[v, 0:HEAD_DIM, :] = qt
        qaug_ref[v, HEAD_DIM:, :] = selb4[v * BIAS_SLOTS:(v + 1) * BIAS_SLOTS].astype(BF16)

    k_idx = lax.broadcasted_iota(jnp.int32, (KV_CHUNK, 1), 0)

    def sel_scores(c):
        k0 = pl.multiple_of(c * KV_CHUNK, KV_CHUNK)
        k_aug = jnp.concatenate([ksel_ref[pl.ds(k0, KV_CHUNK), :], eblk_ref[c % chunks_per_var]],
                                axis=1)
        return _dot(k_aug, qaug_ref[c // chunks_per_var])

    def sel_update(c, sc, carry, causal):
        m_i, l_i, acc = carry
        k0 = pl.multiple_of(c * KV_CHUNK, KV_CHUNK)
        if causal:
            sc = jnp.where((k0 + k_idx) <= t4, sc, NEG_INF)
        m_new = jnp.maximum(m_i, jnp.max(sc, axis=0, keepdims=True))
        alpha = jnp.exp(m_i - m_new)
        pc = jnp.exp(sc - m_new)
        l_new = alpha * l_i + jnp.sum(pc, axis=0, keepdims=True)
        pv = _dot(vselt_ref[:, pl.ds(k0, KV_CHUNK)], pc.astype(BF16))
        return m_new, l_new, alpha * acc + pv

    def sel_group(gi, carry, causal):
        c0 = gi * SEL_UNROLL
        sc = sc_ref[...]
        for u in range(SEL_UNROLL):
            prefetch = not (causal and u + 1 == SEL_UNROLL)
            nxt = sel_scores(c0 + u + 1) if prefetch else None
            carry = sel_update(c0 + u, sc, carry, causal)
            sc = nxt
        if not causal:
            sc_ref[...] = sc
        return carry

    last_group = qb // (SEL_UNROLL * KV_CHUNK // Q_BLOCK)
    init = (jnp.full((1, nq), NEG_INF, F32), jnp.zeros((1, nq), F32),
            jnp.zeros((HEAD_DIM, nq), F32))
    sc_ref[...] = sel_scores(0)
    carry = lax.fori_loop(0, last_group, functools.partial(sel_group, causal=False), init)
    _, l_sel, acc_sel = sel_group(last_group, carry, causal=True)
    o_sel = acc_sel * (1.0 / l_sel)

    gate = jax.nn.sigmoid(gt_ref[...])
    for r in range(NSA_REP):
        cs = slice(r * Q_BLOCK, (r + 1) * Q_BLOCK)
        g0 = gate[N_BRANCH * r + 0:N_BRANCH * r + 1, :]
        g1 = gate[N_BRANCH * r + 1:N_BRANCH * r + 2, :]
        g2 = gate[N_BRANCH * r + 2:N_BRANCH * r + 3, :]
        ot = g0 * o_cmp[:, cs] + g1 * o_sel[:, cs] + g2 * o_win[:, cs]
        o_ref[:, r * HEAD_DIM:(r + 1) * HEAD_DIM] = ot.T


def _block_onehot(n_chunks):
    per_chunk = KV_CHUNK // SEL_LEN
    k = np.arange(KV_CHUNK)[None, :, None]
    e = np.arange(n_chunks)[:, None, None]
    x = np.arange(BIAS_SLOTS)[None, None, :]
    return jnp.asarray((x == e * per_chunk + k // SEL_LEN).astype(np.float32), dtype=BF16)


def _nsa(q, kc, vct, ksw, vt, gt, aggt, batch, seq):
    T = q.shape[0]
    n_qb = seq // Q_BLOCK
    ncp = kc.shape[1] // batch
    nsel = aggt.shape[0]
    gw = NSA_REP * HEAD_DIM
    assert seq % (SEL_UNROLL * KV_CHUNK) == 0
    n_var = pl.cdiv(nsel, BIAS_SLOTS)
    eblk = _block_onehot(min(BIAS_SLOTS * SEL_LEN, seq) // KV_CHUNK)
    big = lambda shape, imap: pl.BlockSpec(shape, imap, pipeline_mode=pl.Buffered(1))
    in_specs = [
        pl.BlockSpec((Q_BLOCK, gw), lambda b, g, i: (b * n_qb + i, g)),
        big((None, ncp, HEAD_DIM), lambda b, g, i: (g, b, 0)),
        big((None, HEAD_DIM, ncp), lambda b, g, i: (g, 0, b)),
        big((seq, HEAD_DIM), lambda b, g, i: (b, g)),
        big((seq, HEAD_DIM), lambda b, g, i: (b, NSA_KV_GROUPS + g)),
        big((HEAD_DIM, seq), lambda b, g, i: (g, b)),
        big((HEAD_DIM, seq), lambda b, g, i: (NSA_KV_GROUPS + g, b)),
        pl.BlockSpec((GATE_ROWS, Q_BLOCK), lambda b, g, i: (g, b * n_qb + i)),
        _const_spec(aggt.shape),
        _const_spec(eblk.shape),
    ]
    return pl.pallas_call(
        _nsa_kernel, grid=(batch, NSA_KV_GROUPS, n_qb), in_specs=in_specs,
        out_specs=pl.BlockSpec((Q_BLOCK, gw), lambda b, g, i: (b * n_qb + i, g)),
        out_shape=jax.ShapeDtypeStruct((T, NSA_HEADS * HEAD_DIM), F32),
        scratch_shapes=[pltpu.VMEM((n_var, HEAD_DIM + BIAS_SLOTS, NSA_REP * Q_BLOCK), BF16),
                        pltpu.VMEM((KV_CHUNK, NSA_REP * Q_BLOCK), F32)],
        compiler_params=_params(("arbitrary", "arbitrary", "arbitrary")),
        name="nsa",
    )(q, kc, vct, ksw, ksw, vt, vt, gt, aggt, eblk)


def _mix_out_kernel(x_ref, on_ref, u_ref, vg_ref, ws_ref, bst_ref, gn_ref, gm_ref, wo_ref,
                    g2_ref, h_ref, y_ref, og_ref):
    tm = x_ref.shape[0]
    row = lax.broadcasted_iota(jnp.int32, (GMLP_CHUNK, GMLP_CHUNK), 0)
    col = lax.broadcasted_iota(jnp.int32, (GMLP_CHUNK, GMLP_CHUNK), 1)
    tril = col <= row
    for h in range(GMLP_GROUPS):
        ws = jnp.where(tril, ws_ref[h], 0.0).astype(BF16)
        bcol = bst_ref[:, h:h + 1]
        cs = slice(h * GMLP_GROUP_DIM, (h + 1) * GMLP_GROUP_DIM)
        for n in range(tm // GMLP_CHUNK):
            rs = slice(n * GMLP_CHUNK, (n + 1) * GMLP_CHUNK)
            mixed = _dot(ws, vg_ref[rs, cs]) + bcol
            og_ref[rs, cs] = u_ref[rs, cs].astype(F32) * mixed

    def rms(v, g):
        return v * lax.rsqrt(jnp.mean(v * v, axis=-1, keepdims=True) + EPS) * g

    half = on_ref.shape[1]
    mix_n = rms(on_ref[...], gn_ref[...]).astype(BF16)
    mix_g = rms(og_ref[...], gm_ref[...]).astype(BF16)
    h1 = x_ref[...] + _dot(mix_n, wo_ref[0:half, :]) + _dot(mix_g, wo_ref[half:, :])
    h_ref[...] = h1
    y_ref[...] = rms(h1, g2_ref[...]).astype(BF16)


def _mix_out(x2, o_nsa, u, vg, ws, bst, gn, gm, wo, g2):
    T, D = x2.shape
    tm = TM_MIX
    half = o_nsa.shape[1]
    tok = lambda w: pl.BlockSpec((tm, w), lambda i: (i, 0))
    in_specs = [tok(D), tok(half), tok(half), tok(half),
                _const_spec(ws.shape), _const_spec(bst.shape),
                _const_spec((1, half)), _const_spec((1, half)),
                _const_spec(wo.shape), _const_spec((1, D))]
    return pl.pallas_call(
        _mix_out_kernel, grid=(T // tm,), in_specs=in_specs,
        out_specs=(tok(D), tok(D)),
        out_shape=(jax.ShapeDtypeStruct((T, D), F32), jax.ShapeDtypeStruct((T, D), BF16)),
        scratch_shapes=[pltpu.VMEM((tm, half), F32)],
        compiler_params=_params(("arbitrary",)),
        name="mix_out",
    )(x2, o_nsa, u, vg, ws, bst, gn, gm, wo, g2)


def _ffn_kernel(y_ref, halo_ref, wg_ref, wu_ref, cwg_ref, cwu_ref, cbg_ref, cbu_ref, wd_ref,
                h_ref, gf_ref, o_ref, ybuf_ref, a_ref, acc_ref, *, tiles_per_seq):
    i = pl.program_id(0)
    j = pl.program_id(1)
    tm = y_ref.shape[0]
    tf = wg_ref.shape[1]
    hr = halo_ref.shape[0]

    @pl.when(j == 0)
    def _():
        keep = jnp.where(i % tiles_per_seq == 0, 0.0, 1.0).astype(BF16)
        ybuf_ref[0:hr] = halo_ref[...] * keep
        ybuf_ref[hr:] = y_ref[...]
        acc_ref[...] = h_ref[...]

    yb = ybuf_ref[...]
    a_ref[:, 0:tf] = _dot(yb, wg_ref[...])
    a_ref[:, tf:] = _dot(yb, wu_ref[...])

    def conv(cols, cw_ref, cb_ref):
        c = cb_ref[...] + cw_ref[CONV_WIDTH - 1:CONV_WIDTH, :] * a_ref[pl.ds(hr, tm), cols]
        for k in range(CONV_WIDTH - 1):
            shift = CONV_WIDTH - 1 - k
            c = c + cw_ref[k:k + 1, :] * a_ref[pl.ds(hr - shift, tm), cols]
        return c

    cg = conv(slice(0, tf), cwg_ref, cbg_ref)
    cu = conv(slice(tf, 2 * tf), cwu_ref, cbu_ref)
    hmid = (cg * jax.nn.sigmoid(cg) * cu).astype(BF16)
    acc_ref[...] += _dot(hmid, wd_ref[...])

    @pl.when(j == pl.num_programs(1) - 1)
    def _():
        hh = acc_ref[...]
        ms = jnp.mean(hh * hh, axis=-1, keepdims=True)
        o_ref[...] = hh * lax.rsqrt(ms + EPS) * gf_ref[...]


def _ffn(y, h1, w_up, conv_w, conv_b, w_down, gf, seq):
    T, D = h1.shape
    dff = w_down.shape[0]
    tm, tf = TM_FFN, TF_FFN
    hr = V7X_BF16_SUBLANE_PACK
    nj = dff // tf
    halo_blocks = tm // hr
    in_specs = [
        pl.BlockSpec((tm, D), lambda i, j: (i, 0)),
        pl.BlockSpec((hr, D), lambda i, j: (jnp.maximum(i * halo_blocks - 1, 0), 0)),
        pl.BlockSpec((D, tf), lambda i, j: (0, j)),
        pl.BlockSpec((D, tf), lambda i, j: (0, nj + j)),
        pl.BlockSpec((CONV_WIDTH, tf), lambda i, j: (0, j)),
        pl.BlockSpec((CONV_WIDTH, tf), lambda i, j: (0, nj + j)),
        pl.BlockSpec((1, tf), lambda i, j: (0, j)),
        pl.BlockSpec((1, tf), lambda i, j: (0, nj + j)),
        pl.BlockSpec((tf, D), lambda i, j: (j, 0)),
        pl.BlockSpec((tm, D), lambda i, j: (i, 0)),
        _const_spec((1, D)),
    ]
    return pl.pallas_call(
        functools.partial(_ffn_kernel, tiles_per_seq=seq // tm),
        grid=(T // tm, nj), in_specs=in_specs,
        out_specs=pl.BlockSpec((tm, D), lambda i, j: (i, 0)),
        out_shape=jax.ShapeDtypeStruct((T, D), F32),
        scratch_shapes=[pltpu.VMEM((tm + hr, D), BF16),
                        pltpu.VMEM((tm + hr, 2 * tf), F32),
                        pltpu.VMEM((tm, D), F32)],
        compiler_params=_params(("arbitrary", "arbitrary")),
        name="ffn",
    )(y, y, w_up, w_up, conv_w, conv_w, conv_b, conv_b, w_down, h1, gf)


def _agg_t(n_sel, ncp, n_cmp):
    c_start = np.arange(ncp) * CMP_STRIDE
    js = np.arange(n_sel)[:, None] * SEL_LEN
    a = (c_start[None, :] < js + SEL_LEN) & (c_start[None, :] + CMP_LEN > js)
    a &= (np.arange(ncp) < n_cmp)[None, :]
    return jnp.asarray(a.astype(np.float32), dtype=BF16)


def _layer(h, positions, norm1_g, w_in, cmp_pe_k, cmp_w_k, cmp_pe_v, cmp_w_v, gmlp_norm_g,
           gmlp_w_s, gmlp_b_s, nsa_out_g, gmlp_out_g, w_out, norm2_g, w_up, conv_w, conv_b,
           w_down, out_g):
    B, S, D = h.shape
    T = B * S
    nq = NSA_HEADS * HEAD_DIM
    kvw = NSA_KV_GROUPS * HEAD_DIM
    x2 = h.reshape(T, D)
    pos2 = positions.reshape(T, 1).astype(jnp.int32)

    half = HEAD_DIM // 2
    inv = ROPE_THETA ** (-2.0 * jnp.arange(half, dtype=F32) / HEAD_DIM)
    inv_full = jnp.concatenate([inv, inv])[None, :]
    sign = jnp.asarray(np.concatenate([-np.ones(half), np.ones(half)])[None, :], dtype=F32)

    kv_w = w_in[:, nq:nq + 6 * kvw].reshape(D, 6, kvw)
    k_cmp_w, v_cmp_w, k_sel_w, v_sel_w, k_win_w, v_win_w = (kv_w[:, i] for i in range(6))
    g_off = nq + 6 * kvw
    n_gate = NSA_HEADS * N_BRANCH
    gate_w = w_in[:, g_off:g_off + n_gate].reshape(D, NSA_KV_GROUPS, NSA_REP * N_BRANCH)
    gate_w = jnp.pad(gate_w, ((0, 0), (0, 0), (0, GATE_ROWS - NSA_REP * N_BRANCH)))
    gate_w = gate_w.reshape(D, NSA_KV_GROUPS * GATE_ROWS)
    u_off = g_off + n_gate
    gw = GMLP_GROUPS * GMLP_GROUP_DIM
    u_w = w_in[:, u_off:u_off + gw]
    v_w = w_in[:, u_off + gw:u_off + 2 * gw]
    wn = jnp.concatenate([w_in[:, :nq], k_sel_w, k_win_w, k_cmp_w, v_cmp_w, u_w, v_w],
                         axis=1).astype(BF16)
    wt = jnp.concatenate([v_sel_w, v_win_w, gate_w], axis=1).T.astype(BF16)

    q, ksw, kvc, u, vg, vt, gt = _inproj(
        x2, pos2, norm1_g.reshape(1, D), inv_full, sign, wn, wt, gmlp_norm_g.reshape(1, gw))

    ncp = S // CMP_STRIDE
    n_cmp = (S - CMP_LEN) // CMP_STRIDE + 1
    hb = CMP_LEN // 2
    kvc2 = kvc.reshape(4, B * ncp, CMP_STRIDE * HEAD_DIM)

    def cmp_weights(w, pe):
        w2 = jnp.concatenate([w[:hb].reshape(hb * HEAD_DIM, HEAD_DIM),
                              w[hb:].reshape(hb * HEAD_DIM, HEAD_DIM)], axis=1).astype(BF16)
        pe8 = jnp.pad(pe.reshape(1, CMP_LEN * HEAD_DIM), ((0, 7), (0, 0))).astype(BF16)
        return w2, pe8, w.reshape(CMP_LEN * HEAD_DIM, HEAD_DIM).astype(BF16)

    posc = jnp.pad(positions[:, CMP_LEN - 1::CMP_STRIDE], ((0, 0), (0, ncp - n_cmp)))
    posc = posc.reshape(B * ncp, 1).astype(jnp.int32)
    kc = _compress(kvc2, *cmp_weights(cmp_w_k, cmp_pe_k), B, rope_args=(posc, inv_full, sign))
    vct = _compress(kvc2, *cmp_weights(cmp_w_v, cmp_pe_v), B)

    o_nsa = _nsa(q, kc, vct, ksw, vt, gt, _agg_t(S // SEL_LEN, ncp, n_cmp), B, S)

    h1, y = _mix_out(x2, o_nsa, u, vg, gmlp_w_s, gmlp_b_s.T, nsa_out_g.reshape(1, nq),
                     gmlp_out_g.reshape(1, gw), w_out.astype(BF16), norm2_g.reshape(1, D))

    out = _ffn(y, h1, w_up.astype(BF16), conv_w, conv_b.reshape(1, -1), w_down.astype(BF16),
               out_g.reshape(1, D), S)
    return out.reshape(B, S, D)


def kernel(x, positions, norm1_g, w_in, cmp_pe_k, cmp_w_k, cmp_pe_v, cmp_w_v, gmlp_norm_g,
           gmlp_w_s, gmlp_b_s, nsa_out_g, gmlp_out_g, w_out, norm2_g, w_up, conv_w, conv_b,
           w_down, final_g):
    depth = norm1_g.shape[0]
    assert depth == 1, "the FFN kernel fuses the final RMSNorm into the only layer"
    return _layer(x, positions, norm1_g[0], w_in[0], cmp_pe_k[0], cmp_w_k[0], cmp_pe_v[0],
                  cmp_w_v[0], gmlp_norm_g[0], gmlp_w_s[0], gmlp_b_s[0], nsa_out_g[0],
                  gmlp_out_g[0], w_out[0], norm2_g[0], w_up[0], conv_w[0], conv_b[0],
                  w_down[0], final_g)
```

```python
import functools

import numpy as np
import jax
import jax.numpy as jnp
from jax import lax
from jax.experimental import pallas as pl
from jax.experimental.pallas import tpu as pltpu

F32 = jnp.float32
BF16 = jnp.bfloat16

HEAD_DIM = 128
NSA_HEADS = 8
NSA_KV_GROUPS = 2
NSA_REP = NSA_HEADS // NSA_KV_GROUPS
N_BRANCH = 3
CMP_LEN = 32
CMP_STRIDE = 16
SEL_LEN = 64
SEL_TOPK = 16
WINDOW = 512
GMLP_GROUP_DIM = 128
GMLP_GROUPS = 8
GMLP_CHUNK = 128
CONV_WIDTH = 3
ROPE_THETA = 10000.0
EPS = 1e-6
Q_BLOCK = 128
NEG_INF = -1e30
LOG2_E = float(np.log2(np.e))
N_FORCED = 3
TOPK_FIRST_QB = SEL_TOPK * SEL_LEN // Q_BLOCK

V7X_LANES = 128
V7X_BF16_SUBLANE_PACK = 16
V7X_VMEM_LIMIT_BYTES = 56 * 1024 * 1024

TM_PROJ = 512
TM_MIX = 512
TM_FFN = 1024
TF_FFN = 512
KV_CHUNK = 512
SEL_UNROLL = 2
BIAS_SLOTS = 128
GATE_ROWS = 16


def _dot(a, b):
    return jnp.dot(a, b, preferred_element_type=F32)


def _dot_nt(a, b):
    return lax.dot_general(a, b, (((1,), (1,)), ((), ())), preferred_element_type=F32)


def _const_spec(shape):
    nd = len(shape)
    return pl.BlockSpec(shape, lambda *_: (0,) * nd, pipeline_mode=pl.Buffered(1))


def _params(semantics):
    return pltpu.CompilerParams(dimension_semantics=semantics,
                                vmem_limit_bytes=V7X_VMEM_LIMIT_BYTES)


def _gelu(x):
    return 0.5 * x * (1.0 + lax.erf(x * (2.0 ** -0.5)))


def _rope(x, cos, sin_signed):
    return x * cos + pltpu.roll(x, HEAD_DIM // 2, 1) * sin_signed


def _inproj_kernel(x_ref, pos_ref, g1_ref, inv_ref, sign_ref, wn_ref, wt_ref, gg_ref,
                   q_ref, ksw_ref, kvc_ref, u_ref, vg_ref, vt_ref, gt_ref):
    x = x_ref[...]
    ms = jnp.mean(x * x, axis=-1, keepdims=True)
    xn = (x * lax.rsqrt(ms + EPS) * g1_ref[...]).astype(BF16)

    ang = pos_ref[...].astype(F32) * inv_ref[...]
    cos = jnp.cos(ang)
    sin_s = jnp.sin(ang) * sign_ref[...]
    scale = HEAD_DIM ** -0.5

    seg_w = 4 * HEAD_DIM

    def seg(i):
        return _dot(xn, wn_ref[:, i * seg_w:(i + 1) * seg_w])

    for i in range(2):
        acc = seg(i)
        for h in range(4):
            qh = _rope(acc[:, h * HEAD_DIM:(h + 1) * HEAD_DIM], cos, sin_s) * scale
            c0 = (i * 4 + h) * HEAD_DIM
            q_ref[:, c0:c0 + HEAD_DIM] = qh.astype(BF16)
    acc = seg(2)
    for h in range(4):
        kh = _rope(acc[:, h * HEAD_DIM:(h + 1) * HEAD_DIM], cos, sin_s)
        ksw_ref[:, h * HEAD_DIM:(h + 1) * HEAD_DIM] = kh.astype(BF16)
    acc = seg(3)
    for s in range(4):
        kvc_ref[s] = acc[:, s * HEAD_DIM:(s + 1) * HEAD_DIM].astype(BF16)
    for i in range(2):
        acc = seg(4 + i)
        u_ref[:, i * seg_w:(i + 1) * seg_w] = _gelu(acc).astype(BF16)
    for i in range(2):
        acc = _gelu(seg(6 + i))
        for h in range(4):
            c0 = i * seg_w + h * GMLP_GROUP_DIM
            vh = acc[:, h * GMLP_GROUP_DIM:(h + 1) * GMLP_GROUP_DIM]
            msv = jnp.mean(vh * vh, axis=-1, keepdims=True)
            vn = vh * lax.rsqrt(msv + EPS) * gg_ref[:, c0:c0 + GMLP_GROUP_DIM]
            vg_ref[:, c0:c0 + GMLP_GROUP_DIM] = vn.astype(BF16)
    rt = _dot_nt(wt_ref[...], xn)
    nv = 4 * HEAD_DIM
    vt_ref[...] = rt[:nv].astype(BF16)
    gt_ref[...] = rt[nv:]


def _inproj(x2, pos2, g1, inv_full, sign, wn, wt, gg):
    T, D = x2.shape
    tm = TM_PROJ
    nt_rows = wt.shape[0]
    grid = (T // tm,)
    tok = lambda w: pl.BlockSpec((tm, w), lambda i: (i, 0))
    out_shape = (
        jax.ShapeDtypeStruct((T, NSA_HEADS * HEAD_DIM), BF16),
        jax.ShapeDtypeStruct((T, 4 * HEAD_DIM), BF16),
        jax.ShapeDtypeStruct((4, T, HEAD_DIM), BF16),
        jax.ShapeDtypeStruct((T, GMLP_GROUPS * GMLP_GROUP_DIM), BF16),
        jax.ShapeDtypeStruct((T, GMLP_GROUPS * GMLP_GROUP_DIM), BF16),
        jax.ShapeDtypeStruct((4 * HEAD_DIM, T), BF16),
        jax.ShapeDtypeStruct((NSA_KV_GROUPS * GATE_ROWS, T), F32),
    )
    out_specs = (
        tok(NSA_HEADS * HEAD_DIM),
        tok(4 * HEAD_DIM),
        pl.BlockSpec((4, tm, HEAD_DIM), lambda i: (0, i, 0)),
        tok(GMLP_GROUPS * GMLP_GROUP_DIM),
        tok(GMLP_GROUPS * GMLP_GROUP_DIM),
        pl.BlockSpec((4 * HEAD_DIM, tm), lambda i: (0, i)),
        pl.BlockSpec((NSA_KV_GROUPS * GATE_ROWS, tm), lambda i: (0, i)),
    )
    in_specs = [
        tok(D),
        pl.BlockSpec((tm, 1), lambda i: (i, 0)),
        _const_spec((1, D)),
        _const_spec((1, HEAD_DIM)),
        _const_spec((1, HEAD_DIM)),
        _const_spec(wn.shape),
        _const_spec((nt_rows, D)),
        _const_spec((1, GMLP_GROUPS * GMLP_GROUP_DIM)),
    ]
    return pl.pallas_call(
        _inproj_kernel, grid=grid, in_specs=in_specs, out_specs=out_specs,
        out_shape=out_shape, compiler_params=_params(("arbitrary",)),
        name="inproj",
    )(x2, pos2, g1, inv_full, sign, wn, wt, gg)


def _compress_body(x_ref, w2_ref, pe_ref, wflat_ref, shift_ref):
    ncp = x_ref.shape[0]
    p = _dot(x_ref[...], w2_ref[...])
    bias = _dot(pe_ref[...], wflat_ref[...])[0:1]
    shift_ref[0:ncp] = p[:, HEAD_DIM:]
    shift_ref[ncp:ncp + 8] = jnp.zeros((8, HEAD_DIM), F32)
    return p[:, :HEAD_DIM] + shift_ref[pl.ds(1, ncp), :] + bias


def _compress_k_kernel(x_ref, w2_ref, pe_ref, wflat_ref, posc_ref, inv_ref, sign_ref,
                       kc_ref, shift_ref):
    kc = _compress_body(x_ref, w2_ref, pe_ref, wflat_ref, shift_ref)
    ang = posc_ref[...].astype(F32) * inv_ref[...]
    kc_ref[...] = _rope(kc, jnp.cos(ang), jnp.sin(ang) * sign_ref[...]).astype(BF16)


def _compress_v_kernel(x_ref, w2_ref, pe_ref, wflat_ref, vct_ref, shift_ref):
    vc = _compress_body(x_ref, w2_ref, pe_ref, wflat_ref, shift_ref)
    vct_ref[...] = vc.T.astype(BF16)


def _compress(kvc, w2, pe8, wflat, batch, rope_args=None):
    ncp = kvc.shape[1] // batch
    kdim = kvc.shape[2]
    grid = (batch, NSA_KV_GROUPS)
    plane0 = 0 if rope_args is not None else NSA_KV_GROUPS
    in_specs = [
        pl.BlockSpec((None, ncp, kdim), lambda b, g: (plane0 + g, b, 0)),
        _const_spec(w2.shape),
        _const_spec(pe8.shape),
        _const_spec(wflat.shape),
    ]
    scratch = [pltpu.VMEM((ncp + 8, HEAD_DIM), F32)]
    if rope_args is not None:
        posc, inv_full, sign = rope_args
        in_specs += [pl.BlockSpec((ncp, 1), lambda b, g: (b, 0)),
                     _const_spec((1, HEAD_DIM)), _const_spec((1, HEAD_DIM))]
        return pl.pallas_call(
            _compress_k_kernel, grid=grid, in_specs=in_specs,
            out_specs=pl.BlockSpec((None, ncp, HEAD_DIM), lambda b, g: (g, b, 0)),
            out_shape=jax.ShapeDtypeStruct((NSA_KV_GROUPS, batch * ncp, HEAD_DIM), BF16),
            scratch_shapes=scratch, compiler_params=_params(("arbitrary", "arbitrary")),
            name="compress_k",
        )(kvc, w2, pe8, wflat, posc, inv_full, sign)
    return pl.pallas_call(
        _compress_v_kernel, grid=grid, in_specs=in_specs,
        out_specs=pl.BlockSpec((None, HEAD_DIM, ncp), lambda b, g: (g, 0, b)),
        out_shape=jax.ShapeDtypeStruct((NSA_KV_GROUPS, HEAD_DIM, batch * ncp), BF16),
        scratch_shapes=scratch, compiler_params=_params(("arbitrary", "arbitrary")),
        name="compress_v",
    )(kvc, w2, pe8, wflat)


def _nsa_kernel(q_ref, kc_ref, vct_ref, ksel_ref, kwin_ref, vselt_ref, vwint_ref, gt_ref,
                aggt_ref, eblk_ref, cbias_ref, wbias_ref, ones_ref, o_ref, qaug_ref, sc_ref):
    qb = pl.program_id(2)
    nq = NSA_REP * Q_BLOCK
    ncp = kc_ref.shape[0]
    nsel = aggt_ref.shape[0]
    n_var = qaug_ref.shape[0]
    chunks_per_var = eblk_ref.shape[0]

    qt_f32 = jnp.concatenate(
        [q_ref[:, r * HEAD_DIM:(r + 1) * HEAD_DIM].astype(F32).T for r in range(NSA_REP)], axis=1)
    qt = qt_f32.astype(BF16)
    qt2 = (qt_f32 * LOG2_E).astype(BF16)
    lane4 = lax.broadcasted_iota(jnp.int32, (1, nq), 1)
    t4 = qb * Q_BLOCK + (lane4 & (Q_BLOCK - 1))

    def pv_aug(vt, pr):
        vt_aug = jnp.concatenate([vt, ones_ref[:, 0:vt.shape[1]]], axis=0)
        return _dot(vt_aug, pr)

    def pv_and_rowsum(vt, pr):
        r_aug = pv_aug(vt, pr)
        return r_aug[0:HEAD_DIM], r_aug[HEAD_DIM:HEAD_DIM + 1]

    s = _dot(kc_ref[...], qt)
    wk = WINDOW + Q_BLOCK
    w0_ = pl.multiple_of(jnp.maximum(qb * Q_BLOCK - WINDOW, 0), Q_BLOCK)
    sw = _dot(kwin_ref[pl.ds(w0_, wk), :], qt2)

    cb0 = pl.multiple_of(ncp - qb * (Q_BLOCK // CMP_STRIDE), Q_BLOCK // CMP_STRIDE)
    s = s + jnp.concatenate([cbias_ref[pl.ds(cb0, ncp), :]] * NSA_REP, axis=1)
    m = jnp.max(s, axis=0, keepdims=True)
    e = jnp.exp(s - m)
    oc, l = pv_and_rowsum(vct_ref[...], e.astype(BF16))
    inv_l = jnp.where(t4 >= CMP_LEN - 1, 1.0 / l, 0.0)
    o_cmp = oc * inv_l

    ps = e[:, 0:Q_BLOCK] * inv_l[:, 0:Q_BLOCK]
    for r in range(1, NSA_REP):
        cs = slice(r * Q_BLOCK, (r + 1) * Q_BLOCK)
        ps = ps + e[:, cs] * inv_l[:, cs]
    ps_hi = ps.astype(BF16)
    ps_lo = (ps - ps_hi.astype(F32)).astype(BF16)
    aggt = aggt_ref[...]
    imp = _dot(aggt, ps_hi) + _dot(aggt, ps_lo)

    wv = jnp.minimum(qb, WINDOW // Q_BLOCK)
    sw = sw + jnp.concatenate([wbias_ref[wv]] * NSA_REP, axis=1)
    mw = jnp.max(sw, axis=0, keepdims=True)
    ew = jnp.exp2(sw - mw)
    ow, lw = pv_and_rowsum(vwint_ref[:, pl.ds(w0_, wk)], ew.astype(BF16))
    o_win = ow * (1.0 / lw)

    jj =lax.broadcasted_iota(jnp.int32, (nsel, Q_BLOCK), 0)
    t1 = qb * Q_BLOCK + lax.broadcasted_iota(jnp.int32, (1, Q_BLOCK), 1)
    cur = t1 >> (SEL_LEN.bit_length() - 1)
    valid = jj <= cur

    forced = (jj == 0) | (jj == cur) | (jj == cur - 1)
    cand = valid & jnp.logical_not(forced)
    lowest = -1.0
    w = jnp.where(cand, imp, lowest)
    jf = jj.astype(F32)
    for _ in range(SEL_TOPK - N_FORCED):
        mx = jnp.max(w, axis=0, keepdims=True)
        idx = jnp.min(jnp.where(w == mx, jf, float(nsel)), axis=0, keepdims=True)
        w = jnp.where(jf == idx, lowest, w)
    picked = forced | (cand & (w == lowest))
    selb = jnp.where(qb < TOPK_FIRST_QB,
                     jnp.where(valid, 0.0, NEG_INF), jnp.where(picked, 0.0, NEG_INF))

    selb4 = jnp.concatenate([selb] * NSA_REP, axis=1)
    pad_rows = n_var * BIAS_SLOTS - nsel
    if pad_rows:
        selb4 = jnp.concatenate([selb4, jnp.zeros((pad_rows, nq), F32)], axis=0)
    for v in range(n_var):
        qaug_ref[v, 0:HEAD_DIM, :] = qt2
        qaug_ref[v, HEAD_DIM:, :] = selb4[v * BIAS_SLOTS:(v + 1) * BIAS_SLOTS].astype(BF16)

    k_idx = lax.broadcasted_iota(jnp.int32, (KV_CHUNK, 1), 0)

    def sel_scores(c):
        k0 = pl.multiple_of(c * KV_CHUNK, KV_CHUNK)
        k_aug = jnp.concatenate([ksel_ref[pl.ds(k0, KV_CHUNK), :], eblk_ref[c % chunks_per_var]],
                                axis=1)
        return _dot(k_aug, qaug_ref[c // chunks_per_var])

    def sel_update(c, sc, carry, causal):
        m_i, acc = carry
        k0 = pl.multiple_of(c * KV_CHUNK, KV_CHUNK)
        if causal:
            sc = jnp.where((k0 + k_idx) <= t4, sc, NEG_INF)
        m_new = jnp.maximum(m_i, jnp.max(sc, axis=0, keepdims=True))
        alpha = jnp.exp2(m_i - m_new)
        pc = jnp.exp2(sc - m_new)
        pv = pv_aug(vselt_ref[:, pl.ds(k0, KV_CHUNK)], pc.astype(BF16))
        return m_new, alpha * acc + pv

    def sel_group(gi, carry, causal):
        c0 = gi * SEL_UNROLL
        sc = sc_ref[...]
        for u in range(SEL_UNROLL):
            prefetch = not (causal and u + 1 == SEL_UNROLL)
            nxt = sel_scores(c0 + u + 1) if prefetch else None
            carry = sel_update(c0 + u, sc, carry, causal)
            sc = nxt
        if not causal:
            sc_ref[...] = sc
        return carry

    last_group = qb // (SEL_UNROLL * KV_CHUNK // Q_BLOCK)
    init = (jnp.full((1, nq), NEG_INF, F32), jnp.zeros((HEAD_DIM + ones_ref.shape[0], nq), F32))
    sc_ref[...] = sel_scores(0)
    carry = lax.fori_loop(0, last_group, functools.partial(sel_group, causal=False), init)
    _, acc_sel = sel_group(last_group, carry, causal=True)
    o_sel = acc_sel[0:HEAD_DIM] * (1.0 / acc_sel[HEAD_DIM:HEAD_DIM + 1])

    gate = jax.nn.sigmoid(gt_ref[...])
    for r in range(NSA_REP):
        cs = slice(r * Q_BLOCK, (r + 1) * Q_BLOCK)
        g0 = gate[N_BRANCH * r + 0:N_BRANCH * r + 1, :]
        g1 = gate[N_BRANCH * r + 1:N_BRANCH * r + 2, :]
        g2 = gate[N_BRANCH * r + 2:N_BRANCH * r + 3, :]
        ot = g0 * o_cmp[:, cs] + g1 * o_sel[:, cs] + g2 * o_win[:, cs]
        o_ref[:, r * HEAD_DIM:(r + 1) * HEAD_DIM] = ot.T


def _block_onehot(n_chunks):
    per_chunk = KV_CHUNK // SEL_LEN
    k = np.arange(KV_CHUNK)[None, :, None]
    e = np.arange(n_chunks)[:, None, None]
    x = np.arange(BIAS_SLOTS)[None, None, :]
    return jnp.asarray((x == e * per_chunk + k // SEL_LEN).astype(np.float32), dtype=BF16)


def _cmp_bias(ncp):
    n_rel = np.arange(2 * ncp)[:, None] - ncp
    q_rel = np.arange(Q_BLOCK)[None, :]
    vis = n_rel * CMP_STRIDE + (CMP_LEN - 1) <= q_rel
    return jnp.asarray(np.where(vis, 0.0, NEG_INF), dtype=F32)


def _win_bias():
    n_var = WINDOW // Q_BLOCK
    base = np.minimum(np.arange(n_var + 1) * Q_BLOCK, WINDOW)[:, None, None]
    d = base + np.arange(Q_BLOCK)[None, None, :] - np.arange(WINDOW + Q_BLOCK)[None, :, None]
    return jnp.asarray(np.where((d >= 0) & (d < WINDOW), 0.0, NEG_INF), dtype=F32)


def _nsa(q, kc, vct, ksw, vt, gt, aggt, batch, seq):
    T = q.shape[0]
    n_qb = seq // Q_BLOCK
    ncp = kc.shape[1] // batch
    nsel = aggt.shape[0]
    gw = NSA_REP * HEAD_DIM
    assert seq % (SEL_UNROLL * KV_CHUNK) == 0
    n_var = pl.cdiv(nsel, BIAS_SLOTS)
    eblk = _block_onehot(min(BIAS_SLOTS * SEL_LEN, seq) // KV_CHUNK)
    cbias = _cmp_bias(ncp)
    wbias = _win_bias()
    ones_w = max(ncp, WINDOW + Q_BLOCK, KV_CHUNK)
    ones_rows = jnp.asarray(np.arange(V7X_BF16_SUBLANE_PACK)[:, None] == 0, dtype=BF16)
    ones_rows = jnp.broadcast_to(ones_rows, (V7X_BF16_SUBLANE_PACK, ones_w))
    big = lambda shape, imap: pl.BlockSpec(shape, imap, pipeline_mode=pl.Buffered(1))
    in_specs = [
        pl.BlockSpec((Q_BLOCK, gw), lambda b, g, i: (b * n_qb + i, g)),
        big((None, ncp, HEAD_DIM), lambda b, g, i: (g, b, 0)),
        big((None, HEAD_DIM, ncp), lambda b, g, i: (g, 0, b)),
        big((seq, HEAD_DIM), lambda b, g, i: (b, g)),
        big((seq, HEAD_DIM), lambda b, g, i: (b, NSA_KV_GROUPS + g)),
        big((HEAD_DIM, seq), lambda b, g, i: (g, b)),
        big((HEAD_DIM, seq), lambda b, g, i: (NSA_KV_GROUPS + g, b)),
        pl.BlockSpec((GATE_ROWS, Q_BLOCK), lambda b, g, i: (g, b * n_qb + i)),
        _const_spec(aggt.shape),
        _const_spec(eblk.shape),
        _const_spec(cbias.shape),
        _const_spec(wbias.shape),
        _const_spec(ones_rows.shape),
    ]
    return pl.pallas_call(
        _nsa_kernel, grid=(batch, NSA_KV_GROUPS, n_qb), in_specs=in_specs,
        out_specs=pl.BlockSpec((Q_BLOCK, gw), lambda b, g, i: (b * n_qb + i, g)),
        out_shape=jax.ShapeDtypeStruct((T, NSA_HEADS * HEAD_DIM), F32),
        scratch_shapes=[pltpu.VMEM((n_var, HEAD_DIM + BIAS_SLOTS, NSA_REP * Q_BLOCK), BF16),
                        pltpu.VMEM((KV_CHUNK, NSA_REP * Q_BLOCK), F32)],
        compiler_params=_params(("arbitrary", "arbitrary", "arbitrary")),
        name="nsa",
    )(q, kc, vct, ksw, ksw, vt, vt, gt, aggt, eblk, cbias, wbias, ones_rows)


def _mix_out_kernel(x_ref, on_ref, u_ref, vg_ref, ws_ref, bst_ref, gn_ref, gm_ref, wo_ref,
                    g2_ref, h_ref, y_ref, og_ref):
    tm = x_ref.shape[0]
    row = lax.broadcasted_iota(jnp.int32, (GMLP_CHUNK, GMLP_CHUNK), 0)
    col = lax.broadcasted_iota(jnp.int32, (GMLP_CHUNK, GMLP_CHUNK), 1)
    tril = col <= row
    for h in range(GMLP_GROUPS):
        ws = jnp.where(tril, ws_ref[h], 0.0).astype(BF16)
        bcol = bst_ref[:, h:h + 1]
        cs = slice(h * GMLP_GROUP_DIM, (h + 1) * GMLP_GROUP_DIM)
        for n in range(tm // GMLP_CHUNK):
            rs = slice(n * GMLP_CHUNK, (n + 1) * GMLP_CHUNK)
            mixed = _dot(ws, vg_ref[rs, cs]) + bcol
            og_ref[rs, cs] = u_ref[rs, cs].astype(F32) * mixed

    def rms(v, g):
        return v * lax.rsqrt(jnp.mean(v * v, axis=-1, keepdims=True) + EPS) * g

    half = on_ref.shape[1]
    mix_n = rms(on_ref[...], gn_ref[...]).astype(BF16)
    mix_g = rms(og_ref[...], gm_ref[...]).astype(BF16)
    h1 = x_ref[...] + _dot(mix_n, wo_ref[0:half, :]) + _dot(mix_g, wo_ref[half:, :])
    h_ref[...] = h1
    y_ref[...] = rms(h1, g2_ref[...]).astype(BF16)


def _mix_out(x2, o_nsa, u, vg, ws, bst, gn, gm, wo, g2):
    T, D = x2.shape
    tm = TM_MIX
    half = o_nsa.shape[1]
    tok = lambda w: pl.BlockSpec((tm, w), lambda i: (i, 0))
    in_specs = [tok(D), tok(half), tok(half), tok(half),
                _const_spec(ws.shape), _const_spec(bst.shape),
                _const_spec((1, half)), _const_spec((1, half)),
                _const_spec(wo.shape), _const_spec((1, D))]
    return pl.pallas_call(
        _mix_out_kernel, grid=(T // tm,), in_specs=in_specs,
        out_specs=(tok(D), tok(D)),
        out_shape=(jax.ShapeDtypeStruct((T, D), F32), jax.ShapeDtypeStruct((T, D), BF16)),
        scratch_shapes=[pltpu.VMEM((tm, half), F32)],
        compiler_params=_params(("arbitrary",)),
        name="mix_out",
    )(x2, o_nsa, u, vg, ws, bst, gn, gm, wo, g2)


def _ffn_kernel(y_ref, halo_ref, wg_ref, wu_ref, cwg_ref, cwu_ref, cbg_ref, cbu_ref, wd_ref,
                h_ref, gf_ref, o_ref, ybuf_ref, a_ref, *, tiles_per_seq):
    i = pl.program_id(0)
    j = pl.program_id(1)
    tm = y_ref.shape[0]
    tf = wg_ref.shape[1]
    hr = halo_ref.shape[0]

    @pl.when(j == 0)
    def _():
        keep = jnp.where(i % tiles_per_seq == 0, 0.0, 1.0).astype(BF16)
        ybuf_ref[0:hr] = halo_ref[...] * keep
        ybuf_ref[hr:] = y_ref[...]
        o_ref[...] = h_ref[...]

    yb = ybuf_ref[...]
    a_ref[:, 0:tf] = _dot(yb, wg_ref[...])
    a_ref[:, tf:] = _dot(yb, wu_ref[...])

    def conv(cols, cw_ref, cb_ref):
        c = cb_ref[...] + cw_ref[CONV_WIDTH - 1:CONV_WIDTH, :] * a_ref[pl.ds(hr, tm), cols]
        for k in range(CONV_WIDTH - 1):
            shift = CONV_WIDTH - 1 - k
            c = c + cw_ref[k:k + 1, :] * a_ref[pl.ds(hr - shift, tm), cols]
        return c

    cg = conv(slice(0, tf), cwg_ref, cbg_ref)
    cu = conv(slice(tf, 2 * tf), cwu_ref, cbu_ref)
    hmid = (cg * jax.nn.sigmoid(cg) * cu).astype(BF16)
    o_ref[...] += _dot(hmid, wd_ref[...])

    @pl.when(j == pl.num_programs(1) - 1)
    def _():
        hh = o_ref[...]
        ms = jnp.mean(hh * hh, axis=-1, keepdims=True)
        o_ref[...] = hh * lax.rsqrt(ms + EPS) * gf_ref[...]


def _ffn(y, h1, w_up, conv_w, conv_b, w_down, gf, seq):
    T, D = h1.shape
    dff = w_down.shape[0]
    tm, tf = TM_FFN, TF_FFN
    hr = V7X_BF16_SUBLANE_PACK
    nj = dff // tf
    halo_blocks = tm // hr
    once = pl.Buffered(1)
    in_specs = [
        pl.BlockSpec((tm, D), lambda i, j: (i, 0), pipeline_mode=once),
        pl.BlockSpec((hr, D), lambda i, j: (jnp.maximum(i * halo_blocks - 1, 0), 0)),
        pl.BlockSpec((D, tf), lambda i, j: (0, j)),
        pl.BlockSpec((D, tf), lambda i, j: (0, nj + j)),
        pl.BlockSpec((CONV_WIDTH, tf), lambda i, j: (0, j)),
        pl.BlockSpec((CONV_WIDTH, tf), lambda i, j: (0, nj + j)),
        pl.BlockSpec((1, tf), lambda i, j: (0, j)),
        pl.BlockSpec((1, tf), lambda i, j: (0, nj + j)),
        pl.BlockSpec((tf, D), lambda i, j: (j, 0)),
        pl.BlockSpec((tm, D), lambda i, j: (i, 0), pipeline_mode=once),
        _const_spec((1, D)),
    ]
    return pl.pallas_call(
        functools.partial(_ffn_kernel, tiles_per_seq=seq // tm),
        grid=(T // tm, nj), in_specs=in_specs,
        out_specs=pl.BlockSpec((tm, D), lambda i, j: (i, 0)),
        out_shape=jax.ShapeDtypeStruct((T, D), F32),
        scratch_shapes=[pltpu.VMEM((tm + hr, D), BF16),
                        pltpu.VMEM((tm + hr, 2 * tf), F32)],
        compiler_params=_params(("arbitrary", "arbitrary")),
        name="ffn",
    )(y, y, w_up, w_up, conv_w, conv_w, conv_b, conv_b, w_down, h1, gf)


def _agg_t(n_sel, ncp, n_cmp):
    c_start = np.arange(ncp) * CMP_STRIDE
    js = np.arange(n_sel)[:, None] * SEL_LEN
    a = (c_start[None, :] < js + SEL_LEN) & (c_start[None, :] + CMP_LEN > js)
    a &= (np.arange(ncp) < n_cmp)[None, :]
    return jnp.asarray(a.astype(np.float32), dtype=BF16)


def _layer(h, positions, norm1_g, w_in, cmp_pe_k, cmp_w_k, cmp_pe_v, cmp_w_v, gmlp_norm_g,
           gmlp_w_s, gmlp_b_s, nsa_out_g, gmlp_out_g, w_out, norm2_g, w_up, conv_w, conv_b,
           w_down, out_g):
    B, S, D = h.shape
    T = B * S
    nq = NSA_HEADS * HEAD_DIM
    kvw = NSA_KV_GROUPS * HEAD_DIM
    x2 = h.reshape(T, D)
    pos2 = positions.reshape(T, 1).astype(jnp.int32)

    half = HEAD_DIM // 2
    inv = ROPE_THETA ** (-2.0 * jnp.arange(half, dtype=F32) / HEAD_DIM)
    inv_full = jnp.concatenate([inv, inv])[None, :]
    sign = jnp.asarray(np.concatenate([-np.ones(half), np.ones(half)])[None, :], dtype=F32)

    kv_w = w_in[:, nq:nq + 6 * kvw].reshape(D, 6, kvw)
    k_cmp_w, v_cmp_w, k_sel_w, v_sel_w, k_win_w, v_win_w = (kv_w[:, i] for i in range(6))
    g_off = nq + 6 * kvw
    n_gate = NSA_HEADS * N_BRANCH
    gate_w = w_in[:, g_off:g_off + n_gate].reshape(D, NSA_KV_GROUPS, NSA_REP * N_BRANCH)
    gate_w = jnp.pad(gate_w, ((0, 0), (0, 0), (0, GATE_ROWS - NSA_REP * N_BRANCH)))
    gate_w = gate_w.reshape(D, NSA_KV_GROUPS * GATE_ROWS)
    u_off = g_off + n_gate
    gw = GMLP_GROUPS * GMLP_GROUP_DIM
    u_w = w_in[:, u_off:u_off + gw]
    v_w = w_in[:, u_off + gw:u_off + 2 * gw]
    wn = jnp.concatenate([w_in[:, :nq], k_sel_w, k_win_w, k_cmp_w, v_cmp_w, u_w, v_w],
                         axis=1).astype(BF16)
    wt = jnp.concatenate([v_sel_w, v_win_w, gate_w], axis=1).T.astype(BF16)

    q, ksw, kvc, u, vg, vt, gt = _inproj(
        x2, pos2, norm1_g.reshape(1, D), inv_full, sign, wn, wt, gmlp_norm_g.reshape(1, gw))

    ncp = S // CMP_STRIDE
    n_cmp = (S - CMP_LEN) // CMP_STRIDE + 1
    hb = CMP_LEN // 2
    kvc2 = kvc.reshape(4, B * ncp, CMP_STRIDE * HEAD_DIM)

    def cmp_weights(w, pe):
        w2 = jnp.concatenate([w[:hb].reshape(hb * HEAD_DIM, HEAD_DIM),
                              w[hb:].reshape(hb * HEAD_DIM, HEAD_DIM)], axis=1).astype(BF16)
        pe8 = jnp.pad(pe.reshape(1, CMP_LEN * HEAD_DIM), ((0, 7), (0, 0))).astype(BF16)
        return w2, pe8, w.reshape(CMP_LEN * HEAD_DIM, HEAD_DIM).astype(BF16)

    posc = jnp.pad(positions[:, CMP_LEN - 1::CMP_STRIDE], ((0, 0), (0, ncp - n_cmp)))
    posc = posc.reshape(B * ncp, 1).astype(jnp.int32)
    kc = _compress(kvc2, *cmp_weights(cmp_w_k, cmp_pe_k), B, rope_args=(posc, inv_full, sign))
    vct = _compress(kvc2, *cmp_weights(cmp_w_v, cmp_pe_v), B)

    o_nsa = _nsa(q, kc, vct, ksw, vt, gt, _agg_t(S // SEL_LEN, ncp, n_cmp), B, S)

    h1, y = _mix_out(x2, o_nsa, u, vg, gmlp_w_s, gmlp_b_s.T, nsa_out_g.reshape(1, nq),
                     gmlp_out_g.reshape(1, gw), w_out.astype(BF16), norm2_g.reshape(1, D))

    out = _ffn(y, h1, w_up.astype(BF16), conv_w, conv_b.reshape(1, -1), w_down.astype(BF16),
               out_g.reshape(1, D), S)
    return out.reshape(B, S, D)


def kernel(x, positions, norm1_g, w_in, cmp_pe_k, cmp_w_k, cmp_pe_v, cmp_w_v, gmlp_norm_g,
           gmlp_w_s, gmlp_b_s, nsa_out_g, gmlp_out_g, w_out, norm2_g, w_up, conv_w, conv_b,
           w_down, final_g):
    depth = norm1_g.shape[0]
    assert depth == 1, "the FFN kernel fuses the final RMSNorm into the only layer"
    return _layer(x, positions, norm1_g[0], w_in[0], cmp_pe_k[0], cmp_w_k[0], cmp_pe_v[0],
                  cmp_w_v[0], gmlp_norm_g[0], gmlp_w_s[0], gmlp_b_s[0], nsa_out_g[0],
                  gmlp_out_g[0], w_out[0], norm2_g[0], w_up[0], conv_w[0], conv_b[0],
                  w_down[0], final_g)
```

```python
import functools

import numpy as np
import jax
import jax.numpy as jnp
from jax import lax
from jax.experimental import pallas as pl
from jax.experimental.pallas import tpu as pltpu

F32 = jnp.float32
BF16 = jnp.bfloat16

HEAD_DIM = 128
NSA_HEADS = 8
NSA_KV_GROUPS = 2
NSA_REP = NSA_HEADS // NSA_KV_GROUPS
N_BRANCH = 3
CMP_LEN = 32
CMP_STRIDE = 16
SEL_LEN = 64
SEL_TOPK = 16
WINDOW = 512
GMLP_GROUP_DIM = 128
GMLP_GROUPS = 8
GMLP_CHUNK = 128
CONV_WIDTH = 3
ROPE_THETA = 10000.0
EPS = 1e-6
Q_BLOCK = 128
NEG_INF = -1e30
LOG2_E = float(np.log2(np.e))
FIXED_SHIFT_LIMIT = 1e30
N_FORCED = 3
TOPK_FIRST_QB = SEL_TOPK * SEL_LEN // Q_BLOCK

V7X_LANES = 128
V7X_BF16_SUBLANE_PACK = 16
V7X_VMEM_LIMIT_BYTES = 56 * 1024 * 1024

TM_PROJ = 512
TM_MIX = 512
TM_FFN = 512
TF_FFN = 512
KV_CHUNK = 512
SEL_UNROLL = 4
BIAS_SLOTS = 128
GATE_ROWS = 16


def _dot(a, b):
    return jnp.dot(a, b, preferred_element_type=F32)


def _dot_nt(a, b):
    return lax.dot_general(a, b, (((1,), (1,)), ((), ())), preferred_element_type=F32)


def _const_spec(shape):
    nd = len(shape)
    return pl.BlockSpec(shape, lambda *_: (0,) * nd, pipeline_mode=pl.Buffered(1))


def _params(semantics):
    return pltpu.CompilerParams(dimension_semantics=semantics,
                                vmem_limit_bytes=V7X_VMEM_LIMIT_BYTES)


def _gelu(x):
    return 0.5 * x * (1.0 + lax.erf(x * (2.0 ** -0.5)))


def _rope(x, cos, sin_signed):
    return x * cos + pltpu.roll(x, HEAD_DIM // 2, 1) * sin_signed


def _inproj_kernel(x_ref, pos_ref, g1_ref, inv_ref, sign_ref, wn_ref, wt_ref, gg_ref,
                   q_ref, ksw_ref, kvc_ref, u_ref, vg_ref, vt_ref, gt_ref):
    x = x_ref[...]
    ms = jnp.mean(x * x, axis=-1, keepdims=True)
    xn = (x * lax.rsqrt(ms + EPS) * g1_ref[...]).astype(BF16)

    ang = pos_ref[...].astype(F32) * inv_ref[...]
    cos = jnp.cos(ang)
    sin_s = jnp.sin(ang) * sign_ref[...]
    scale = HEAD_DIM ** -0.5

    seg_w = 4 * HEAD_DIM

    def seg(i):
        return _dot(xn, wn_ref[:, i * seg_w:(i + 1) * seg_w])

    for i in range(2):
        acc = seg(i)
        for h in range(4):
            qh = _rope(acc[:, h * HEAD_DIM:(h + 1) * HEAD_DIM], cos, sin_s) * scale
            c0 = (i * 4 + h) * HEAD_DIM
            q_ref[:, c0:c0 + HEAD_DIM] = qh.astype(BF16)
    acc = seg(2)
    for h in range(4):
        kh = _rope(acc[:, h * HEAD_DIM:(h + 1) * HEAD_DIM], cos, sin_s)
        ksw_ref[:, h * HEAD_DIM:(h + 1) * HEAD_DIM] = kh.astype(BF16)
    acc = seg(3)
    for s in range(4):
        kvc_ref[s] = acc[:, s * HEAD_DIM:(s + 1) * HEAD_DIM].astype(BF16)
    for i in range(2):
        acc = seg(4 + i)
        u_ref[:, i * seg_w:(i + 1) * seg_w] = _gelu(acc).astype(BF16)
    for i in range(2):
        acc = _gelu(seg(6 + i))
        for h in range(4):
            c0 = i * seg_w + h * GMLP_GROUP_DIM
            vh = acc[:, h * GMLP_GROUP_DIM:(h + 1) * GMLP_GROUP_DIM]
            msv = jnp.mean(vh * vh, axis=-1, keepdims=True)
            vn = vh * lax.rsqrt(msv + EPS) * gg_ref[:, c0:c0 + GMLP_GROUP_DIM]
            vg_ref[:, c0:c0 + GMLP_GROUP_DIM] = vn.astype(BF16)
    rt = _dot_nt(wt_ref[...], xn)
    nv = 4 * HEAD_DIM
    vt_ref[...] = rt[:nv].astype(BF16)
    gt_ref[...] = rt[nv:]


def _inproj(x2, pos2, g1, inv_full, sign, wn, wt, gg):
    T, D = x2.shape
    tm = TM_PROJ
    nt_rows = wt.shape[0]
    grid = (T // tm,)
    tok = lambda w: pl.BlockSpec((tm, w), lambda i: (i, 0))
    out_shape = (
        jax.ShapeDtypeStruct((T, NSA_HEADS * HEAD_DIM), BF16),
        jax.ShapeDtypeStruct((T, 4 * HEAD_DIM), BF16),
        jax.ShapeDtypeStruct((4, T, HEAD_DIM), BF16),
        jax.ShapeDtypeStruct((T, GMLP_GROUPS * GMLP_GROUP_DIM), BF16),
        jax.ShapeDtypeStruct((T, GMLP_GROUPS * GMLP_GROUP_DIM), BF16),
        jax.ShapeDtypeStruct((4 * HEAD_DIM, T), BF16),
        jax.ShapeDtypeStruct((NSA_KV_GROUPS * GATE_ROWS, T), F32),
    )
    out_specs = (
        tok(NSA_HEADS * HEAD_DIM),
        tok(4 * HEAD_DIM),
        pl.BlockSpec((4, tm, HEAD_DIM), lambda i: (0, i, 0)),
        tok(GMLP_GROUPS * GMLP_GROUP_DIM),
        tok(GMLP_GROUPS * GMLP_GROUP_DIM),
        pl.BlockSpec((4 * HEAD_DIM, tm), lambda i: (0, i)),
        pl.BlockSpec((NSA_KV_GROUPS * GATE_ROWS, tm), lambda i: (0, i)),
    )
    in_specs = [
        tok(D),
        pl.BlockSpec((tm, 1), lambda i: (i, 0)),
        _const_spec((1, D)),
        _const_spec((1, HEAD_DIM)),
        _const_spec((1, HEAD_DIM)),
        _const_spec(wn.shape),
        _const_spec((nt_rows, D)),
        _const_spec((1, GMLP_GROUPS * GMLP_GROUP_DIM)),
    ]
    return pl.pallas_call(
        _inproj_kernel, grid=grid, in_specs=in_specs, out_specs=out_specs,
        out_shape=out_shape, compiler_params=_params(("arbitrary",)),
        name="inproj",
    )(x2, pos2, g1, inv_full, sign, wn, wt, gg)


def _compress_body(x_ref, w2_ref, pe_ref, wflat_ref, shift_ref):
    ncp = x_ref.shape[0]
    p = _dot(x_ref[...], w2_ref[...])
    bias = _dot(pe_ref[...], wflat_ref[...])[0:1]
    shift_ref[0:ncp] = p[:, HEAD_DIM:]
    shift_ref[ncp:ncp + 8] = jnp.zeros((8, HEAD_DIM), F32)
    return p[:, :HEAD_DIM] + shift_ref[pl.ds(1, ncp), :] + bias


def _compress_k_kernel(x_ref, w2_ref, pe_ref, wflat_ref, posc_ref, inv_ref, sign_ref,
                       kc_ref, shift_ref):
    kc = _compress_body(x_ref, w2_ref, pe_ref, wflat_ref, shift_ref)
    ang = posc_ref[...].astype(F32) * inv_ref[...]
    kc_ref[...] = _rope(kc, jnp.cos(ang), jnp.sin(ang) * sign_ref[...]).astype(BF16)


def _compress_v_kernel(x_ref, w2_ref, pe_ref, wflat_ref, vct_ref, shift_ref):
    vc = _compress_body(x_ref, w2_ref, pe_ref, wflat_ref, shift_ref)
    vct_ref[...] = vc.T.astype(BF16)


def _compress(kvc, w2, pe8, wflat, batch, rope_args=None):
    ncp = kvc.shape[1] // batch
    kdim = kvc.shape[2]
    grid = (batch, NSA_KV_GROUPS)
    plane0 = 0 if rope_args is not None else NSA_KV_GROUPS
    in_specs = [
        pl.BlockSpec((None, ncp, kdim), lambda b, g: (plane0 + g, b, 0)),
        _const_spec(w2.shape),
        _const_spec(pe8.shape),
        _const_spec(wflat.shape),
    ]
    scratch = [pltpu.VMEM((ncp + 8, HEAD_DIM), F32)]
    if rope_args is not None:
        posc, inv_full, sign = rope_args
        in_specs += [pl.BlockSpec((ncp, 1), lambda b, g: (b, 0)),
                     _const_spec((1, HEAD_DIM)), _const_spec((1, HEAD_DIM))]
        return pl.pallas_call(
            _compress_k_kernel, grid=grid, in_specs=in_specs,
            out_specs=pl.BlockSpec((None, ncp, HEAD_DIM), lambda b, g: (g, b, 0)),
            out_shape=jax.ShapeDtypeStruct((NSA_KV_GROUPS, batch * ncp, HEAD_DIM), BF16),
            scratch_shapes=scratch, compiler_params=_params(("arbitrary", "arbitrary")),
            name="compress_k",
        )(kvc, w2, pe8, wflat, posc, inv_full, sign)
    return pl.pallas_call(
        _compress_v_kernel, grid=grid, in_specs=in_specs,
        out_specs=pl.BlockSpec((None, HEAD_DIM, ncp), lambda b, g: (g, 0, b)),
        out_shape=jax.ShapeDtypeStruct((NSA_KV_GROUPS, HEAD_DIM, batch * ncp), BF16),
        scratch_shapes=scratch, compiler_params=_params(("arbitrary", "arbitrary")),
        name="compress_v",
    )(kvc, w2, pe8, wflat)


def _nsa_kernel(q_ref, kc_ref, vct_ref, ksel_ref, kwin_ref, vselt_ref, vwint_ref, gt_ref,
                aggt_ref, eblk_ref, cbias_ref, wbias_ref, ones_ref, o_ref, qaug_ref, sc_ref):
    qb = pl.program_id(2)
    nq = NSA_REP * Q_BLOCK
    ncp = kc_ref.shape[0]
    nsel = aggt_ref.shape[0]
    n_var = qaug_ref.shape[0]
    chunks_per_var = eblk_ref.shape[0]

    qt_f32 = jnp.concatenate(
        [q_ref[:, r * HEAD_DIM:(r + 1) * HEAD_DIM].astype(F32).T for r in range(NSA_REP)], axis=1)
    qt = qt_f32.astype(BF16)
    qt2 = (qt_f32 * LOG2_E).astype(BF16)
    lane4 = lax.broadcasted_iota(jnp.int32, (1, nq), 1)
    t4 = qb * Q_BLOCK + (lane4 & (Q_BLOCK - 1))

    def pv_aug(vt, pr):
        vt_aug = jnp.concatenate([vt, ones_ref[:, 0:vt.shape[1]]], axis=0)
        return _dot(vt_aug, pr)

    def pv_and_rowsum(vt, pr):
        r_aug = pv_aug(vt, pr)
        return r_aug[0:HEAD_DIM], r_aug[HEAD_DIM:HEAD_DIM + 1]

    s = _dot(kc_ref[...], qt)
    wk = WINDOW + Q_BLOCK
    w0_ = pl.multiple_of(jnp.maximum(qb * Q_BLOCK - WINDOW, 0), Q_BLOCK)
    sw = _dot(kwin_ref[pl.ds(w0_, wk), :], qt2)

    cb0 = pl.multiple_of(ncp - qb * (Q_BLOCK // CMP_STRIDE), Q_BLOCK // CMP_STRIDE)
    s = s + jnp.concatenate([cbias_ref[pl.ds(cb0, ncp), :]] * NSA_REP, axis=1)
    m = jnp.max(s, axis=0, keepdims=True)
    e = jnp.exp(s - m)
    oc, l = pv_and_rowsum(vct_ref[...], e.astype(BF16))
    inv_l = jnp.where(t4 >= CMP_LEN - 1, 1.0 / l, 0.0)
    o_cmp = oc * inv_l

    ps = e[:, 0:Q_BLOCK] * inv_l[:, 0:Q_BLOCK]
    for r in range(1, NSA_REP):
        cs = slice(r * Q_BLOCK, (r + 1) * Q_BLOCK)
        ps = ps + e[:, cs] * inv_l[:, cs]
    ps_hi = ps.astype(BF16)
    ps_lo = (ps - ps_hi.astype(F32)).astype(BF16)
    aggt = aggt_ref[...]
    imp = _dot(aggt, ps_hi) + _dot(aggt, ps_lo)

    wv = jnp.minimum(qb, WINDOW // Q_BLOCK)
    sw = sw + jnp.concatenate([wbias_ref[wv]] * NSA_REP, axis=1)
    mw = jnp.max(sw, axis=0, keepdims=True)
    ew = jnp.exp2(sw - mw)
    ow, lw = pv_and_rowsum(vwint_ref[:, pl.ds(w0_, wk)], ew.astype(BF16))
    o_win = ow * (1.0 / lw)

    jj =lax.broadcasted_iota(jnp.int32, (nsel, Q_BLOCK), 0)
    t1 = qb * Q_BLOCK + lax.broadcasted_iota(jnp.int32, (1, Q_BLOCK), 1)
    cur = t1 >> (SEL_LEN.bit_length() - 1)
    valid = jj <= cur

    forced = (jj == 0) | (jj == cur) | (jj == cur - 1)
    cand = valid & jnp.logical_not(forced)
    lowest = -1.0
    w = jnp.where(cand, imp, lowest)
    jf = jj.astype(F32)
    for _ in range(SEL_TOPK - N_FORCED):
        mx = jnp.max(w, axis=0, keepdims=True)
        idx = jnp.min(jnp.where(w == mx, jf, float(nsel)), axis=0, keepdims=True)
        w = jnp.where(jf == idx, lowest, w)
    picked = forced | (cand & (w == lowest))
    selb = jnp.where(qb < TOPK_FIRST_QB,
                     jnp.where(valid, 0.0, NEG_INF), jnp.where(picked, 0.0, NEG_INF))

    selb4 = jnp.concatenate([selb] * NSA_REP, axis=1)
    pad_rows = n_var * BIAS_SLOTS - nsel
    if pad_rows:
        selb4 = jnp.concatenate([selb4, jnp.zeros((pad_rows, nq), F32)], axis=0)
    for v in range(n_var):
        qaug_ref[v, 0:HEAD_DIM, :] = qt2
        qaug_ref[v, HEAD_DIM:, :] = selb4[v * BIAS_SLOTS:(v + 1) * BIAS_SLOTS].astype(BF16)

    k_idx = lax.broadcasted_iota(jnp.int32, (KV_CHUNK, 1), 0)

    def sel_scores(c):
        k0 = pl.multiple_of(c * KV_CHUNK, KV_CHUNK)
        k_aug = jnp.concatenate([ksel_ref[pl.ds(k0, KV_CHUNK), :], eblk_ref[c % chunks_per_var]],
                                axis=1)
        return _dot(k_aug, qaug_ref[c // chunks_per_var])

    def causal_mask(c, sc):
        return jnp.where((c * KV_CHUNK + k_idx) <= t4, sc, NEG_INF)

    def sel_pv(c, pc):
        k0 = pl.multiple_of(c * KV_CHUNK, KV_CHUNK)
        return pv_aug(vselt_ref[:, pl.ds(k0, KV_CHUNK)], pc.astype(BF16))

    def online_update(c, sc, carry, causal):
        m_i, acc = carry
        if causal:
            sc = causal_mask(c, sc)
        m_new = jnp.maximum(m_i, jnp.max(sc, axis=0, keepdims=True))
        alpha = jnp.exp2(m_i - m_new)
        return m_new, alpha * acc + sel_pv(c, jnp.exp2(sc - m_new))

    def sel_group(gi, carry, causal, update):
        c0 = gi * SEL_UNROLL
        sc = sc_ref[...]
        for u in range(SEL_UNROLL):
            prefetch = not (causal and u + 1 == SEL_UNROLL)
            nxt = sel_scores(c0 + u + 1) if prefetch else None
            carry = update(c0 + u, sc, carry, causal)
            sc = nxt
        if not causal:
            sc_ref[...] = sc
        return carry

    def sel_branch(update, init, sc_first):
        last_group = qb // (SEL_UNROLL * KV_CHUNK // Q_BLOCK)
        sc_ref[...] = sc_first
        carry = lax.fori_loop(
            0, last_group, functools.partial(sel_group, causal=False, update=update), init)
        return sel_group(last_group, carry, True, update)

    acc0 = jnp.zeros((HEAD_DIM + ones_ref.shape[0], nq), F32)

    sc0 = sel_scores(0)
    m0 = jnp.max(causal_mask(0, sc0), axis=0, keepdims=True)

    def fixed_shift_update(c, sc, acc, causal):
        if causal:
            sc = causal_mask(c, sc)
        return acc + sel_pv(c, jnp.exp2(sc - m0))

    acc_fast = sel_branch(fixed_shift_update, acc0, sc0)

    in_range = jnp.max(jnp.abs(acc_fast)) < FIXED_SHIFT_LIMIT

    def online_branch():
        init = (jnp.full((1, nq), NEG_INF, F32), acc0)
        return sel_branch(online_update, init, sel_scores(0))[1]

    acc_sel = lax.cond(in_range, lambda: acc_fast, online_branch)
    o_sel = acc_sel[0:HEAD_DIM] * (1.0 / acc_sel[HEAD_DIM:HEAD_DIM + 1])

    gate = jax.nn.sigmoid(gt_ref[...])
    for r in range(NSA_REP):
        cs = slice(r * Q_BLOCK, (r + 1) * Q_BLOCK)
        g0 = gate[N_BRANCH * r + 0:N_BRANCH * r + 1, :]
        g1 = gate[N_BRANCH * r + 1:N_BRANCH * r + 2, :]
        g2 = gate[N_BRANCH * r + 2:N_BRANCH * r + 3, :]
        ot = g0 * o_cmp[:, cs] + g1 * o_sel[:, cs] + g2 * o_win[:, cs]
        o_ref[:, r * HEAD_DIM:(r + 1) * HEAD_DIM] = ot.T


def _block_onehot(n_chunks):
    per_chunk = KV_CHUNK // SEL_LEN
    k = np.arange(KV_CHUNK)[None, :, None]
    e = np.arange(n_chunks)[:, None, None]
    x = np.arange(BIAS_SLOTS)[None, None, :]
    return jnp.asarray((x == e * per_chunk + k // SEL_LEN).astype(np.float32), dtype=BF16)


def _cmp_bias(ncp):
    n_rel = np.arange(2 * ncp)[:, None] - ncp
    q_rel = np.arange(Q_BLOCK)[None, :]
    vis = n_rel * CMP_STRIDE + (CMP_LEN - 1) <= q_rel
    return jnp.asarray(np.where(vis, 0.0, NEG_INF), dtype=F32)


def _win_bias():
    n_var = WINDOW // Q_BLOCK
    base = np.minimum(np.arange(n_var + 1) * Q_BLOCK, WINDOW)[:, None, None]
    d = base + np.arange(Q_BLOCK)[None, None, :] - np.arange(WINDOW + Q_BLOCK)[None, :, None]
    return jnp.asarray(np.where((d >= 0) & (d < WINDOW), 0.0, NEG_INF), dtype=F32)


def _nsa(q, kc, vct, ksw, vt, gt, aggt, batch, seq):
    T = q.shape[0]
    n_qb = seq // Q_BLOCK
    ncp = kc.shape[1] // batch
    nsel = aggt.shape[0]
    gw = NSA_REP * HEAD_DIM
    assert seq % (SEL_UNROLL * KV_CHUNK) == 0
    n_var = pl.cdiv(nsel, BIAS_SLOTS)
    eblk = _block_onehot(min(BIAS_SLOTS * SEL_LEN, seq) // KV_CHUNK)
    cbias = _cmp_bias(ncp)
    wbias = _win_bias()
    ones_w = max(ncp, WINDOW + Q_BLOCK, KV_CHUNK)
    ones_rows = jnp.asarray(np.arange(V7X_BF16_SUBLANE_PACK)[:, None] == 0, dtype=BF16)
    ones_rows = jnp.broadcast_to(ones_rows, (V7X_BF16_SUBLANE_PACK, ones_w))
    big = lambda shape, imap: pl.BlockSpec(shape, imap, pipeline_mode=pl.Buffered(1))
    in_specs = [
        pl.BlockSpec((Q_BLOCK, gw), lambda b, g, i: (b * n_qb + i, g)),
        big((None, ncp, HEAD_DIM), lambda b, g, i: (g, b, 0)),
        big((None, HEAD_DIM, ncp), lambda b, g, i: (g, 0, b)),
        big((seq, HEAD_DIM), lambda b, g, i: (b, g)),
        big((seq, HEAD_DIM), lambda b, g, i: (b, NSA_KV_GROUPS + g)),
        big((HEAD_DIM, seq), lambda b, g, i: (g, b)),
        big((HEAD_DIM, seq), lambda b, g, i: (NSA_KV_GROUPS + g, b)),
        pl.BlockSpec((GATE_ROWS, Q_BLOCK), lambda b, g, i: (g, b * n_qb + i)),
        _const_spec(aggt.shape),
        _const_spec(eblk.shape),
        _const_spec(cbias.shape),
        _const_spec(wbias.shape),
        _const_spec(ones_rows.shape),
    ]
    return pl.pallas_call(
        _nsa_kernel, grid=(batch, NSA_KV_GROUPS, n_qb), in_specs=in_specs,
        out_specs=pl.BlockSpec((Q_BLOCK, gw), lambda b, g, i: (b * n_qb + i, g)),
        out_shape=jax.ShapeDtypeStruct((T, NSA_HEADS * HEAD_DIM), F32),
        scratch_shapes=[pltpu.VMEM((n_var, HEAD_DIM + BIAS_SLOTS, NSA_REP * Q_BLOCK), BF16),
                        pltpu.VMEM((KV_CHUNK, NSA_REP * Q_BLOCK), F32)],
        compiler_params=_params(("arbitrary", "arbitrary", "arbitrary")),
        name="nsa",
    )(q, kc, vct, ksw, ksw, vt, vt, gt, aggt, eblk, cbias, wbias, ones_rows)


def _mix_out_kernel(x_ref, on_ref, u_ref, vg_ref, ws_ref, bst_ref, gn_ref, gm_ref, wo_ref,
                    g2_ref, h_ref, y_ref, og_ref):
    tm = x_ref.shape[0]
    row = lax.broadcasted_iota(jnp.int32, (GMLP_CHUNK, GMLP_CHUNK), 0)
    col = lax.broadcasted_iota(jnp.int32, (GMLP_CHUNK, GMLP_CHUNK), 1)
    tril = col <= row
    for h in range(GMLP_GROUPS):
        ws = jnp.where(tril, ws_ref[h], 0.0).astype(BF16)
        bcol = bst_ref[:, h:h + 1]
        cs = slice(h * GMLP_GROUP_DIM, (h + 1) * GMLP_GROUP_DIM)
        for n in range(tm // GMLP_CHUNK):
            rs = slice(n * GMLP_CHUNK, (n + 1) * GMLP_CHUNK)
            mixed = _dot(ws, vg_ref[rs, cs]) + bcol
            og_ref[rs, cs] = u_ref[rs, cs].astype(F32) * mixed

    def rms(v, g):
        return v * lax.rsqrt(jnp.mean(v * v, axis=-1, keepdims=True) + EPS) * g

    half = on_ref.shape[1]
    mix_n = rms(on_ref[...], gn_ref[...]).astype(BF16)
    mix_g = rms(og_ref[...], gm_ref[...]).astype(BF16)
    h1 = x_ref[...] + _dot(mix_n, wo_ref[0:half, :]) + _dot(mix_g, wo_ref[half:, :])
    h_ref[...] = h1
    y_ref[...] = rms(h1, g2_ref[...]).astype(BF16)


def _mix_out(x2, o_nsa, u, vg, ws, bst, gn, gm, wo, g2):
    T, D = x2.shape
    tm = TM_MIX
    half = o_nsa.shape[1]
    tok = lambda w: pl.BlockSpec((tm, w), lambda i: (i, 0))
    in_specs = [tok(D), tok(half), tok(half), tok(half),
                _const_spec(ws.shape), _const_spec(bst.shape),
                _const_spec((1, half)), _const_spec((1, half)),
                _const_spec(wo.shape), _const_spec((1, D))]
    return pl.pallas_call(
        _mix_out_kernel, grid=(T // tm,), in_specs=in_specs,
        out_specs=(tok(D), tok(D)),
        out_shape=(jax.ShapeDtypeStruct((T, D), F32), jax.ShapeDtypeStruct((T, D), BF16)),
        scratch_shapes=[pltpu.VMEM((tm, half), F32)],
        compiler_params=_params(("arbitrary",)),
        name="mix_out",
    )(x2, o_nsa, u, vg, ws, bst, gn, gm, wo, g2)


def _ffn_kernel(y_ref, halo_ref, wg_ref, wu_ref, cwg_ref, cwu_ref, cbg_ref, cbu_ref, wd_ref,
                h_ref, gf_ref, o_ref, ybuf_ref, a_ref, acc_ref, *, tiles_per_seq):
    i = pl.program_id(0)
    j = pl.program_id(1)
    tm = y_ref.shape[0]
    tf = wg_ref.shape[1]
    hr = halo_ref.shape[0]

    @pl.when(j == 0)
    def _():
        keep = jnp.where(i % tiles_per_seq == 0, 0.0, 1.0).astype(BF16)
        ybuf_ref[0:hr] = halo_ref[...] * keep
        ybuf_ref[hr:] = y_ref[...]
        acc_ref[...] = h_ref[...]

    yb = ybuf_ref[...]
    a_ref[:, 0:tf] = _dot(yb, wg_ref[...])
    a_ref[:, tf:] = _dot(yb, wu_ref[...])

    def conv(cols, cw_ref, cb_ref):
        c = cb_ref[...] + cw_ref[CONV_WIDTH - 1:CONV_WIDTH, :] * a_ref[pl.ds(hr, tm), cols]
        for k in range(CONV_WIDTH - 1):
            shift = CONV_WIDTH - 1 - k
            c = c + cw_ref[k:k + 1, :] * a_ref[pl.ds(hr - shift, tm), cols]
        return c

    cg = conv(slice(0, tf), cwg_ref, cbg_ref)
    cu = conv(slice(tf, 2 * tf), cwu_ref, cbu_ref)
    hmid = (cg * jax.nn.sigmoid(cg) * cu).astype(BF16)
    acc_ref[...] += _dot(hmid, wd_ref[...])

    @pl.when(j == pl.num_programs(1) - 1)
    def _():
        hh = acc_ref[...]
        ms = jnp.mean(hh * hh, axis=-1, keepdims=True)
        o_ref[...] = hh * lax.rsqrt(ms + EPS) * gf_ref[...]


def _ffn(y, h1, w_up, conv_w, conv_b, w_down, gf, seq):
    T, D = h1.shape
    dff = w_down.shape[0]
    tm, tf = TM_FFN, TF_FFN
    hr = V7X_BF16_SUBLANE_PACK
    nj = dff // tf
    halo_blocks = tm // hr
    in_specs = [
        pl.BlockSpec((tm, D), lambda i, j: (i, 0)),
        pl.BlockSpec((hr, D), lambda i, j: (jnp.maximum(i * halo_blocks - 1, 0), 0)),
        pl.BlockSpec((D, tf), lambda i, j: (0, j)),
        pl.BlockSpec((D, tf), lambda i, j: (0, nj + j)),
        pl.BlockSpec((CONV_WIDTH, tf), lambda i, j: (0, j)),
        pl.BlockSpec((CONV_WIDTH, tf), lambda i, j: (0, nj + j)),
        pl.BlockSpec((1, tf), lambda i, j: (0, j)),
        pl.BlockSpec((1, tf), lambda i, j: (0, nj + j)),
        pl.BlockSpec((tf, D), lambda i, j: (j, 0)),
        pl.BlockSpec((tm, D), lambda i, j: (i, 0)),
        _const_spec((1, D)),
    ]
    return pl.pallas_call(
        functools.partial(_ffn_kernel, tiles_per_seq=seq // tm),
        grid=(T // tm, nj), in_specs=in_specs,
        out_specs=pl.BlockSpec((tm, D), lambda i, j: (i, 0)),
        out_shape=jax.ShapeDtypeStruct((T, D), F32),
        scratch_shapes=[pltpu.VMEM((tm + hr, D), BF16),
                        pltpu.VMEM((tm + hr, 2 * tf), F32),
                        pltpu.VMEM((tm, D), F32)],
        compiler_params=_params(("arbitrary", "arbitrary")),
        name="ffn",
    )(y, y, w_up, w_up, conv_w, conv_w, conv_b, conv_b, w_down, h1, gf)


def _agg_t(n_sel, ncp, n_cmp):
    c_start = np.arange(ncp) * CMP_STRIDE
    js = np.arange(n_sel)[:, None] * SEL_LEN
    a = (c_start[None, :] < js + SEL_LEN) & (c_start[None, :] + CMP_LEN > js)
    a &= (np.arange(ncp) < n_cmp)[None, :]
    return jnp.asarray(a.astype(np.float32), dtype=BF16)


def _layer(h, positions, norm1_g, w_in, cmp_pe_k, cmp_w_k, cmp_pe_v, cmp_w_v, gmlp_norm_g,
           gmlp_w_s, gmlp_b_s, nsa_out_g, gmlp_out_g, w_out, norm2_g, w_up, conv_w, conv_b,
           w_down, out_g):
    B, S, D = h.shape
    T = B * S
    nq = NSA_HEADS * HEAD_DIM
    kvw = NSA_KV_GROUPS * HEAD_DIM
    x2 = h.reshape(T, D)
    pos2 = positions.reshape(T, 1).astype(jnp.int32)

    half = HEAD_DIM // 2
    inv = ROPE_THETA ** (-2.0 * jnp.arange(half, dtype=F32) / HEAD_DIM)
    inv_full = jnp.concatenate([inv, inv])[None, :]
    sign = jnp.asarray(np.concatenate([-np.ones(half), np.ones(half)])[None, :], dtype=F32)

    kv_w = w_in[:, nq:nq + 6 * kvw].reshape(D, 6, kvw)
    k_cmp_w, v_cmp_w, k_sel_w, v_sel_w, k_win_w, v_win_w = (kv_w[:, i] for i in range(6))
    g_off = nq + 6 * kvw
    n_gate = NSA_HEADS * N_BRANCH
    gate_w = w_in[:, g_off:g_off + n_gate].reshape(D, NSA_KV_GROUPS, NSA_REP * N_BRANCH)
    gate_w = jnp.pad(gate_w, ((0, 0), (0, 0), (0, GATE_ROWS - NSA_REP * N_BRANCH)))
    gate_w = gate_w.reshape(D, NSA_KV_GROUPS * GATE_ROWS)
    u_off = g_off + n_gate
    gw = GMLP_GROUPS * GMLP_GROUP_DIM
    u_w = w_in[:, u_off:u_off + gw]
    v_w = w_in[:, u_off + gw:u_off + 2 * gw]
    wn = jnp.concatenate([w_in[:, :nq], k_sel_w, k_win_w, k_cmp_w, v_cmp_w, u_w, v_w],
                         axis=1).astype(BF16)
    wt = jnp.concatenate([v_sel_w, v_win_w, gate_w], axis=1).T.astype(BF16)

    q, ksw, kvc, u, vg, vt, gt = _inproj(
        x2, pos2, norm1_g.reshape(1, D), inv_full, sign, wn, wt, gmlp_norm_g.reshape(1, gw))

    ncp = S // CMP_STRIDE
    n_cmp = (S - CMP_LEN) // CMP_STRIDE + 1
    hb = CMP_LEN // 2
    kvc2 = kvc.reshape(4, B * ncp, CMP_STRIDE * HEAD_DIM)

    def cmp_weights(w, pe):
        w2 = jnp.concatenate([w[:hb].reshape(hb * HEAD_DIM, HEAD_DIM),
                              w[hb:].reshape(hb * HEAD_DIM, HEAD_DIM)], axis=1).astype(BF16)
        pe8 = jnp.pad(pe.reshape(1, CMP_LEN * HEAD_DIM), ((0, 7), (0, 0))).astype(BF16)
        return w2, pe8, w.reshape(CMP_LEN * HEAD_DIM, HEAD_DIM).astype(BF16)

    posc = jnp.pad(positions[:, CMP_LEN - 1::CMP_STRIDE], ((0, 0), (0, ncp - n_cmp)))
    posc = posc.reshape(B * ncp, 1).astype(jnp.int32)
    kc = _compress(kvc2, *cmp_weights(cmp_w_k, cmp_pe_k), B, rope_args=(posc, inv_full, sign))
    vct = _compress(kvc2, *cmp_weights(cmp_w_v, cmp_pe_v), B)

    o_nsa = _nsa(q, kc, vct, ksw, vt, gt, _agg_t(S // SEL_LEN, ncp, n_cmp), B, S)

    h1, y = _mix_out(x2, o_nsa, u, vg, gmlp_w_s, gmlp_b_s.T, nsa_out_g.reshape(1, nq),
                     gmlp_out_g.reshape(1, gw), w_out.astype(BF16), norm2_g.reshape(1, D))

    out = _ffn(y, h1, w_up.astype(BF16), conv_w, conv_b.reshape(1, -1), w_down.astype(BF16),
               out_g.reshape(1, D), S)
    return out.reshape(B, S, D)


def kernel(x, positions, norm1_g, w_in, cmp_pe_k, cmp_w_k, cmp_pe_v, cmp_w_v, gmlp_norm_g,
           gmlp_w_s, gmlp_b_s, nsa_out_g, gmlp_out_g, w_out, norm2_g, w_up, conv_w, conv_b,
           w_down, final_g):
    depth = norm1_g.shape[0]
    assert depth == 1, "the FFN kernel fuses the final RMSNorm into the only layer"
    return _layer(x, positions, norm1_g[0], w_in[0], cmp_pe_k[0], cmp_w_k[0], cmp_pe_v[0],
                  cmp_w_v[0], gmlp_norm_g[0], gmlp_w_s[0], gmlp_b_s[0], nsa_out_g[0],
                  gmlp_out_g[0], w_out[0], norm2_g[0], w_up[0], conv_w[0], conv_b[0],
                  w_down[0], final_g)
```

```python
import functools

import numpy as np
import jax
import jax.numpy as jnp
from jax import lax
from jax.experimental import pallas as pl
from jax.experimental.pallas import tpu as pltpu

F32 = jnp.float32
BF16 = jnp.bfloat16

HEAD_DIM = 128
NSA_HEADS = 8
NSA_KV_GROUPS = 2
NSA_REP = NSA_HEADS // NSA_KV_GROUPS
N_BRANCH = 3
CMP_LEN = 32
CMP_STRIDE = 16
SEL_LEN = 64
SEL_TOPK = 16
WINDOW = 512
GMLP_GROUP_DIM = 128
GMLP_GROUPS = 8
GMLP_CHUNK = 128
CONV_WIDTH = 3
ROPE_THETA = 10000.0
EPS = 1e-6
Q_BLOCK = 128
NEG_INF = -1e30
LOG2_E = float(np.log2(np.e))
FIXED_SHIFT_LIMIT = 1e30
N_FORCED = 3
TOPK_FIRST_QB = SEL_TOPK * SEL_LEN // Q_BLOCK

V7X_LANES = 128
V7X_BF16_SUBLANE_PACK = 16
V7X_VMEM_LIMIT_BYTES = 56 * 1024 * 1024

TM_PROJ = 512
TM_MIX = 512
TM_FFN = 512
TF_FFN = 512
KV_CHUNK = 512
SEL_UNROLL = 4
BIAS_SLOTS = 128
GATE_ROWS = 16


def _dot(a, b):
    return jnp.dot(a, b, preferred_element_type=F32)


def _dot_nt(a, b):
    return lax.dot_general(a, b, (((1,), (1,)), ((), ())), preferred_element_type=F32)


def _const_spec(shape):
    nd = len(shape)
    return pl.BlockSpec(shape, lambda *_: (0,) * nd, pipeline_mode=pl.Buffered(1))


def _params(semantics):
    return pltpu.CompilerParams(dimension_semantics=semantics,
                                vmem_limit_bytes=V7X_VMEM_LIMIT_BYTES)


def _gelu(x):
    return 0.5 * x * (1.0 + lax.erf(x * (2.0 ** -0.5)))


def _rope(x, cos, sin_signed):
    return x * cos + pltpu.roll(x, HEAD_DIM // 2, 1) * sin_signed


def _inproj_kernel(x_ref, pos_ref, g1_ref, inv_ref, sign_ref, wn_ref, wt_ref, gg_ref,
                   q_ref, ksw_ref, kvc_ref, u_ref, vg_ref, vt_ref, gt_ref):
    x = x_ref[...]
    ms = jnp.mean(x * x, axis=-1, keepdims=True)
    xn = (x * lax.rsqrt(ms + EPS) * g1_ref[...]).astype(BF16)

    ang = pos_ref[...].astype(F32) * inv_ref[...]
    cos = jnp.cos(ang)
    sin_s = jnp.sin(ang) * sign_ref[...]
    scale = HEAD_DIM ** -0.5

    seg_w = 4 * HEAD_DIM

    def seg(i):
        return _dot(xn, wn_ref[:, i * seg_w:(i + 1) * seg_w])

    for i in range(2):
        acc = seg(i)
        for h in range(4):
            qh = _rope(acc[:, h * HEAD_DIM:(h + 1) * HEAD_DIM], cos, sin_s) * scale
            c0 = (i * 4 + h) * HEAD_DIM
            q_ref[:, c0:c0 + HEAD_DIM] = qh.astype(BF16)
    acc = seg(2)
    for h in range(4):
        kh = _rope(acc[:, h * HEAD_DIM:(h + 1) * HEAD_DIM], cos, sin_s)
        ksw_ref[:, h * HEAD_DIM:(h + 1) * HEAD_DIM] = kh.astype(BF16)
    acc = seg(3)
    for s in range(4):
        kvc_ref[s] = acc[:, s * HEAD_DIM:(s + 1) * HEAD_DIM].astype(BF16)
    for i in range(2):
        acc = seg(4 + i)
        u_ref[:, i * seg_w:(i + 1) * seg_w] = _gelu(acc).astype(BF16)
    for i in range(2):
        acc = _gelu(seg(6 + i))
        for h in range(4):
            c0 = i * seg_w + h * GMLP_GROUP_DIM
            vh = acc[:, h * GMLP_GROUP_DIM:(h + 1) * GMLP_GROUP_DIM]
            msv = jnp.mean(vh * vh, axis=-1, keepdims=True)
            vn = vh * lax.rsqrt(msv + EPS) * gg_ref[:, c0:c0 + GMLP_GROUP_DIM]
            vg_ref[:, c0:c0 + GMLP_GROUP_DIM] = vn.astype(BF16)
    rt = _dot_nt(wt_ref[...], xn)
    nv = 4 * HEAD_DIM
    vt_ref[...] = rt[:nv].astype(BF16)
    gt_ref[...] = rt[nv:]


def _inproj(x2, pos2, g1, inv_full, sign, wn, wt, gg):
    T, D = x2.shape
    tm = TM_PROJ
    nt_rows = wt.shape[0]
    grid = (T // tm,)
    tok = lambda w: pl.BlockSpec((tm, w), lambda i: (i, 0))
    out_shape = (
        jax.ShapeDtypeStruct((T, NSA_HEADS * HEAD_DIM), BF16),
        jax.ShapeDtypeStruct((T, 4 * HEAD_DIM), BF16),
        jax.ShapeDtypeStruct((4, T, HEAD_DIM), BF16),
        jax.ShapeDtypeStruct((T, GMLP_GROUPS * GMLP_GROUP_DIM), BF16),
        jax.ShapeDtypeStruct((T, GMLP_GROUPS * GMLP_GROUP_DIM), BF16),
        jax.ShapeDtypeStruct((4 * HEAD_DIM, T), BF16),
        jax.ShapeDtypeStruct((NSA_KV_GROUPS * GATE_ROWS, T), F32),
    )
    out_specs = (
        tok(NSA_HEADS * HEAD_DIM),
        tok(4 * HEAD_DIM),
        pl.BlockSpec((4, tm, HEAD_DIM), lambda i: (0, i, 0)),
        tok(GMLP_GROUPS * GMLP_GROUP_DIM),
        tok(GMLP_GROUPS * GMLP_GROUP_DIM),
        pl.BlockSpec((4 * HEAD_DIM, tm), lambda i: (0, i)),
        pl.BlockSpec((NSA_KV_GROUPS * GATE_ROWS, tm), lambda i: (0, i)),
    )
    in_specs = [
        tok(D),
        pl.BlockSpec((tm, 1), lambda i: (i, 0)),
        _const_spec((1, D)),
        _const_spec((1, HEAD_DIM)),
        _const_spec((1, HEAD_DIM)),
        _const_spec(wn.shape),
        _const_spec((nt_rows, D)),
        _const_spec((1, GMLP_GROUPS * GMLP_GROUP_DIM)),
    ]
    return pl.pallas_call(
        _inproj_kernel, grid=grid, in_specs=in_specs, out_specs=out_specs,
        out_shape=out_shape, compiler_params=_params(("arbitrary",)),
        name="inproj",
    )(x2, pos2, g1, inv_full, sign, wn, wt, gg)


def _compress_body(x_ref, w2_ref, pe_ref, wflat_ref, shift_ref):
    ncp = x_ref.shape[0]
    p = _dot(x_ref[...], w2_ref[...])
    bias = _dot(pe_ref[...], wflat_ref[...])[0:1]
    shift_ref[0:ncp] = p[:, HEAD_DIM:]
    shift_ref[ncp:ncp + 8] = jnp.zeros((8, HEAD_DIM), F32)
    return p[:, :HEAD_DIM] + shift_ref[pl.ds(1, ncp), :] + bias


def _compress_k_kernel(x_ref, w2_ref, pe_ref, wflat_ref, posc_ref, inv_ref, sign_ref,
                       kc_ref, shift_ref):
    kc = _compress_body(x_ref, w2_ref, pe_ref, wflat_ref, shift_ref)
    ang = posc_ref[...].astype(F32) * inv_ref[...]
    kc_ref[...] = _rope(kc, jnp.cos(ang), jnp.sin(ang) * sign_ref[...]).astype(BF16)


def _compress_v_kernel(x_ref, w2_ref, pe_ref, wflat_ref, vct_ref, shift_ref):
    vc = _compress_body(x_ref, w2_ref, pe_ref, wflat_ref, shift_ref)
    vct_ref[...] = vc.T.astype(BF16)


def _compress(kvc, w2, pe8, wflat, batch, rope_args=None):
    ncp = kvc.shape[1] // batch
    kdim = kvc.shape[2]
    grid = (batch, NSA_KV_GROUPS)
    plane0 = 0 if rope_args is not None else NSA_KV_GROUPS
    in_specs = [
        pl.BlockSpec((None, ncp, kdim), lambda b, g: (plane0 + g, b, 0)),
        _const_spec(w2.shape),
        _const_spec(pe8.shape),
        _const_spec(wflat.shape),
    ]
    scratch = [pltpu.VMEM((ncp + 8, HEAD_DIM), F32)]
    if rope_args is not None:
        posc, inv_full, sign = rope_args
        in_specs += [pl.BlockSpec((ncp, 1), lambda b, g: (b, 0)),
                     _const_spec((1, HEAD_DIM)), _const_spec((1, HEAD_DIM))]
        return pl.pallas_call(
            _compress_k_kernel, grid=grid, in_specs=in_specs,
            out_specs=pl.BlockSpec((None, ncp, HEAD_DIM), lambda b, g: (g, b, 0)),
            out_shape=jax.ShapeDtypeStruct((NSA_KV_GROUPS, batch * ncp, HEAD_DIM), BF16),
            scratch_shapes=scratch, compiler_params=_params(("arbitrary", "arbitrary")),
            name="compress_k",
        )(kvc, w2, pe8, wflat, posc, inv_full, sign)
    return pl.pallas_call(
        _compress_v_kernel, grid=grid, in_specs=in_specs,
        out_specs=pl.BlockSpec((None, HEAD_DIM, ncp), lambda b, g: (g, 0, b)),
        out_shape=jax.ShapeDtypeStruct((NSA_KV_GROUPS, HEAD_DIM, batch * ncp), BF16),
        scratch_shapes=scratch, compiler_params=_params(("arbitrary", "arbitrary")),
        name="compress_v",
    )(kvc, w2, pe8, wflat)


def _nsa_kernel(q_ref, kc_ref, vct_ref, ksel_ref, kwin_ref, vselt_ref, vwint_ref, gt_ref,
                aggt_ref, eblk_ref, cbias_ref, wbias_ref, ones_ref, o_ref, qaug_ref, sc_ref):
    qb = pl.program_id(2)
    nq = NSA_REP * Q_BLOCK
    ncp = kc_ref.shape[0]
    nsel = aggt_ref.shape[0]
    n_var = qaug_ref.shape[0]
    chunks_per_var = eblk_ref.shape[0]

    qt_f32 = jnp.concatenate(
        [q_ref[:, r * HEAD_DIM:(r + 1) * HEAD_DIM].astype(F32).T for r in range(NSA_REP)], axis=1)
    qt = qt_f32.astype(BF16)
    qt2 = (qt_f32 * LOG2_E).astype(BF16)
    lane4 = lax.broadcasted_iota(jnp.int32, (1, nq), 1)
    t4 = qb * Q_BLOCK + (lane4 & (Q_BLOCK - 1))

    def pv_aug(vt, pr):
        vt_aug = jnp.concatenate([vt, ones_ref[:, 0:vt.shape[1]]], axis=0)
        return _dot(vt_aug, pr)

    def pv_and_rowsum(vt, pr):
        r_aug = pv_aug(vt, pr)
        return r_aug[0:HEAD_DIM], r_aug[HEAD_DIM:HEAD_DIM + 1]

    def compressed_window_select(nc, ns):
        s = _dot(kc_ref[0:nc, :], qt)
        wk = WINDOW + Q_BLOCK
        w0_ = pl.multiple_of(jnp.maximum(qb * Q_BLOCK - WINDOW, 0), Q_BLOCK)
        sw = _dot(kwin_ref[pl.ds(w0_, wk), :], qt2)

        cb0 = pl.multiple_of(ncp - qb * (Q_BLOCK // CMP_STRIDE), Q_BLOCK // CMP_STRIDE)
        s = s + jnp.concatenate([cbias_ref[pl.ds(cb0, nc), :]] * NSA_REP, axis=1)
        m = jnp.max(s, axis=0, keepdims=True)
        e = jnp.exp(s - m)
        oc, l = pv_and_rowsum(vct_ref[:, 0:nc], e.astype(BF16))
        inv_l = jnp.where(t4 >= CMP_LEN - 1, 1.0 / l, 0.0)
        o_c = oc * inv_l

        ps = e[:, 0:Q_BLOCK] * inv_l[:, 0:Q_BLOCK]
        for r in range(1, NSA_REP):
            cs = slice(r * Q_BLOCK, (r + 1) * Q_BLOCK)
            ps = ps + e[:, cs] * inv_l[:, cs]
        ps_hi = ps.astype(BF16)
        ps_lo = (ps - ps_hi.astype(F32)).astype(BF16)
        aggt = aggt_ref[0:ns, 0:nc]
        imp = _dot(aggt, ps_hi) + _dot(aggt, ps_lo)

        wv = jnp.minimum(qb, WINDOW // Q_BLOCK)
        sw = sw + jnp.concatenate([wbias_ref[wv]] * NSA_REP, axis=1)
        mw = jnp.max(sw, axis=0, keepdims=True)
        ew = jnp.exp2(sw - mw)
        ow, lw = pv_and_rowsum(vwint_ref[:, pl.ds(w0_, wk)], ew.astype(BF16))
        o_w = ow * (1.0 / lw)

        jj = lax.broadcasted_iota(jnp.int32, (ns, Q_BLOCK), 0)
        t1 = qb * Q_BLOCK + lax.broadcasted_iota(jnp.int32, (1, Q_BLOCK), 1)
        cur = t1 >> (SEL_LEN.bit_length() - 1)
        valid = jj <= cur

        forced = (jj == 0) | (jj == cur) | (jj == cur - 1)
        cand = valid & jnp.logical_not(forced)
        lowest = -1.0
        w = jnp.where(cand, imp, lowest)
        jf = jj.astype(F32)
        for _ in range(SEL_TOPK - N_FORCED):
            mx = jnp.max(w, axis=0, keepdims=True)
            idx = jnp.min(jnp.where(w == mx, jf, float(ns)), axis=0, keepdims=True)
            w = jnp.where(jf == idx, lowest, w)
        picked = forced | (cand & (w == lowest))
        sb = jnp.where(qb < TOPK_FIRST_QB,
                       jnp.where(valid, 0.0, NEG_INF), jnp.where(picked, 0.0, NEG_INF))
        if ns < nsel:
            sb = jnp.concatenate([sb, jnp.full((nsel - ns, Q_BLOCK), NEG_INF, F32)], axis=0)
        return o_c, sb, o_w

    if (ncp // 2) % V7X_LANES == 0 and (nsel // 2) % V7X_BF16_SUBLANE_PACK == 0:
        o_cmp, selb, o_win = lax.cond(qb < nsel // 4,
                                      lambda: compressed_window_select(ncp // 2, nsel // 2),
                                      lambda: compressed_window_select(ncp, nsel))
    else:
        o_cmp, selb, o_win = compressed_window_select(ncp, nsel)

    selb4 = jnp.concatenate([selb] * NSA_REP, axis=1)
    pad_rows = n_var * BIAS_SLOTS - nsel
    if pad_rows:
        selb4 = jnp.concatenate([selb4, jnp.zeros((pad_rows, nq), F32)], axis=0)
    for v in range(n_var):
        qaug_ref[v, 0:HEAD_DIM, :] = qt2
        qaug_ref[v, HEAD_DIM:, :] = selb4[v * BIAS_SLOTS:(v + 1) * BIAS_SLOTS].astype(BF16)

    k_idx = lax.broadcasted_iota(jnp.int32, (KV_CHUNK, 1), 0)

    def sel_scores(c):
        k0 = pl.multiple_of(c * KV_CHUNK, KV_CHUNK)
        k_aug = jnp.concatenate([ksel_ref[pl.ds(k0, KV_CHUNK), :], eblk_ref[c % chunks_per_var]],
                                axis=1)
        return _dot(k_aug, qaug_ref[c // chunks_per_var])

    def causal_mask(c, sc):
        return jnp.where((c * KV_CHUNK + k_idx) <= t4, sc, NEG_INF)

    def sel_pv(c, pc):
        k0 = pl.multiple_of(c * KV_CHUNK, KV_CHUNK)
        return pv_aug(vselt_ref[:, pl.ds(k0, KV_CHUNK)], pc.astype(BF16))

    def online_update(c, sc, carry, causal):
        m_i, acc = carry
        if causal:
            sc = causal_mask(c, sc)
        m_new = jnp.maximum(m_i, jnp.max(sc, axis=0, keepdims=True))
        alpha = jnp.exp2(m_i - m_new)
        return m_new, alpha * acc + sel_pv(c, jnp.exp2(sc - m_new))

    def sel_chunks(c0, carry, update, n, last):
        sc = sc_ref[...]
        for u in range(n):
            nxt = None if last else sel_scores(c0 + u + 1)
            carry = update(c0 + u, sc, carry, last)
            sc = nxt
        if not last:
            sc_ref[...] = sc
        return carry

    def sel_branch(update, init, sc_first):
        diag = qb // (KV_CHUNK // Q_BLOCK)
        n_groups = diag // SEL_UNROLL
        sc_ref[...] = sc_first
        carry = lax.fori_loop(
            0, n_groups,
            lambda gi, cy: sel_chunks(gi * SEL_UNROLL, cy, update, SEL_UNROLL, False), init)
        carry = lax.fori_loop(
            n_groups * SEL_UNROLL, diag, lambda c, cy: sel_chunks(c, cy, update, 1, False), carry)
        return sel_chunks(diag, carry, update, 1, True)

    acc0 = jnp.zeros((HEAD_DIM + ones_ref.shape[0], nq), F32)

    sc0 = sel_scores(0)
    m0 = sc0[0:1, :]

    def fixed_shift_update(c, sc, acc, causal):
        if causal:
            sc = causal_mask(c, sc)
        return acc + sel_pv(c, jnp.exp2(sc - m0))

    acc_fast = sel_branch(fixed_shift_update, acc0, sc0)

    in_range = jnp.max(jnp.abs(acc_fast)) < FIXED_SHIFT_LIMIT

    def online_branch():
        init = (jnp.full((1, nq), NEG_INF, F32), acc0)
        return sel_branch(online_update, init, sel_scores(0))[1]

    acc_sel = lax.cond(in_range, lambda: acc_fast, online_branch)
    o_sel = acc_sel[0:HEAD_DIM] * (1.0 / acc_sel[HEAD_DIM:HEAD_DIM + 1])

    gate = jax.nn.sigmoid(gt_ref[...])
    for r in range(NSA_REP):
        cs = slice(r * Q_BLOCK, (r + 1) * Q_BLOCK)
        g0 = gate[N_BRANCH * r + 0:N_BRANCH * r + 1, :]
        g1 = gate[N_BRANCH * r + 1:N_BRANCH * r + 2, :]
        g2 = gate[N_BRANCH * r + 2:N_BRANCH * r + 3, :]
        ot = g0 * o_cmp[:, cs] + g1 * o_sel[:, cs] + g2 * o_win[:, cs]
        o_ref[:, r * HEAD_DIM:(r + 1) * HEAD_DIM] = ot.T


def _block_onehot(n_chunks):
    per_chunk = KV_CHUNK // SEL_LEN
    k = np.arange(KV_CHUNK)[None, :, None]
    e = np.arange(n_chunks)[:, None, None]
    x = np.arange(BIAS_SLOTS)[None, None, :]
    return jnp.asarray((x == e * per_chunk + k // SEL_LEN).astype(np.float32), dtype=BF16)


def _cmp_bias(ncp):
    n_rel = np.arange(2 * ncp)[:, None] - ncp
    q_rel = np.arange(Q_BLOCK)[None, :]
    vis = n_rel * CMP_STRIDE + (CMP_LEN - 1) <= q_rel
    return jnp.asarray(np.where(vis, 0.0, NEG_INF), dtype=F32)


def _win_bias():
    n_var = WINDOW // Q_BLOCK
    base = np.minimum(np.arange(n_var + 1) * Q_BLOCK, WINDOW)[:, None, None]
    d = base + np.arange(Q_BLOCK)[None, None, :] - np.arange(WINDOW + Q_BLOCK)[None, :, None]
    return jnp.asarray(np.where((d >= 0) & (d < WINDOW), 0.0, NEG_INF), dtype=F32)


def _nsa(q, kc, vct, ksw, vt, gt, aggt, batch, seq):
    T = q.shape[0]
    n_qb = seq // Q_BLOCK
    ncp = kc.shape[1] // batch
    nsel = aggt.shape[0]
    gw = NSA_REP * HEAD_DIM
    assert seq % KV_CHUNK == 0
    n_var = pl.cdiv(nsel, BIAS_SLOTS)
    eblk = _block_onehot(min(BIAS_SLOTS * SEL_LEN, seq) // KV_CHUNK)
    cbias = _cmp_bias(ncp)
    wbias = _win_bias()
    ones_w = max(ncp, WINDOW + Q_BLOCK, KV_CHUNK)
    ones_rows = jnp.asarray(np.arange(V7X_BF16_SUBLANE_PACK)[:, None] == 0, dtype=BF16)
    ones_rows = jnp.broadcast_to(ones_rows, (V7X_BF16_SUBLANE_PACK, ones_w))
    big = lambda shape, imap: pl.BlockSpec(shape, imap, pipeline_mode=pl.Buffered(1))
    in_specs = [
        pl.BlockSpec((Q_BLOCK, gw), lambda b, g, i: (b * n_qb + i, g)),
        big((None, ncp, HEAD_DIM), lambda b, g, i: (g, b, 0)),
        big((None, HEAD_DIM, ncp), lambda b, g, i: (g, 0, b)),
        big((seq, HEAD_DIM), lambda b, g, i: (b, g)),
        big((seq, HEAD_DIM), lambda b, g, i: (b, NSA_KV_GROUPS + g)),
        big((HEAD_DIM, seq), lambda b, g, i: (g, b)),
        big((HEAD_DIM, seq), lambda b, g, i: (NSA_KV_GROUPS + g, b)),
        pl.BlockSpec((GATE_ROWS, Q_BLOCK), lambda b, g, i: (g, b * n_qb + i)),
        _const_spec(aggt.shape),
        _const_spec(eblk.shape),
        _const_spec(cbias.shape),
        _const_spec(wbias.shape),
        _const_spec(ones_rows.shape),
    ]
    return pl.pallas_call(
        _nsa_kernel, grid=(batch, NSA_KV_GROUPS, n_qb), in_specs=in_specs,
        out_specs=pl.BlockSpec((Q_BLOCK, gw), lambda b, g, i: (b * n_qb + i, g)),
        out_shape=jax.ShapeDtypeStruct((T, NSA_HEADS * HEAD_DIM), F32),
        scratch_shapes=[pltpu.VMEM((n_var, HEAD_DIM + BIAS_SLOTS, NSA_REP * Q_BLOCK), BF16),
                        pltpu.VMEM((KV_CHUNK, NSA_REP * Q_BLOCK), F32)],
        compiler_params=_params(("arbitrary", "arbitrary", "arbitrary")),
        name="nsa",
    )(q, kc, vct, ksw, ksw, vt, vt, gt, aggt, eblk, cbias, wbias, ones_rows)


def _mix_out_kernel(x_ref, on_ref, u_ref, vg_ref, ws_ref, bst_ref, gn_ref, gm_ref, wo_ref,
                    g2_ref, h_ref, y_ref, og_ref):
    tm = x_ref.shape[0]
    row = lax.broadcasted_iota(jnp.int32, (GMLP_CHUNK, GMLP_CHUNK), 0)
    col = lax.broadcasted_iota(jnp.int32, (GMLP_CHUNK, GMLP_CHUNK), 1)
    tril = col <= row
    for h in range(GMLP_GROUPS):
        ws = jnp.where(tril, ws_ref[h], 0.0).astype(BF16)
        bcol = bst_ref[:, h:h + 1]
        cs = slice(h * GMLP_GROUP_DIM, (h + 1) * GMLP_GROUP_DIM)
        for n in range(tm // GMLP_CHUNK):
            rs = slice(n * GMLP_CHUNK, (n + 1) * GMLP_CHUNK)
            mixed = _dot(ws, vg_ref[rs, cs]) + bcol
            og_ref[rs, cs] = u_ref[rs, cs].astype(F32) * mixed

    def rms(v, g):
        return v * lax.rsqrt(jnp.mean(v * v, axis=-1, keepdims=True) + EPS) * g

    half = on_ref.shape[1]
    mix_n = rms(on_ref[...], gn_ref[...]).astype(BF16)
    mix_g = rms(og_ref[...], gm_ref[...]).astype(BF16)
    h1 = x_ref[...] + _dot(mix_n, wo_ref[0:half, :]) + _dot(mix_g, wo_ref[half:, :])
    h_ref[...] = h1
    y_ref[...] = rms(h1, g2_ref[...]).astype(BF16)


def _mix_out(x2, o_nsa, u, vg, ws, bst, gn, gm, wo, g2):
    T, D = x2.shape
    tm = TM_MIX
    half = o_nsa.shape[1]
    tok = lambda w: pl.BlockSpec((tm, w), lambda i: (i, 0))
    in_specs = [tok(D), tok(half), tok(half), tok(half),
                _const_spec(ws.shape), _const_spec(bst.shape),
                _const_spec((1, half)), _const_spec((1, half)),
                _const_spec(wo.shape), _const_spec((1, D))]
    return pl.pallas_call(
        _mix_out_kernel, grid=(T // tm,), in_specs=in_specs,
        out_specs=(tok(D), tok(D)),
        out_shape=(jax.ShapeDtypeStruct((T, D), F32), jax.ShapeDtypeStruct((T, D), BF16)),
        scratch_shapes=[pltpu.VMEM((tm, half), F32)],
        compiler_params=_params(("arbitrary",)),
        name="mix_out",
    )(x2, o_nsa, u, vg, ws, bst, gn, gm, wo, g2)


def _ffn_kernel(y_ref, halo_ref, wg_ref, wu_ref, cwg_ref, cwu_ref, cbg_ref, cbu_ref, wd_ref,
                h_ref, gf_ref, o_ref, ybuf_ref, a_ref, acc_ref, *, tiles_per_seq):
    i = pl.program_id(0)
    j = pl.program_id(1)
    tm = y_ref.shape[0]
    tf = wg_ref.shape[1]
    hr = halo_ref.shape[0]

    @pl.when(j == 0)
    def _():
        keep = jnp.where(i % tiles_per_seq == 0, 0.0, 1.0).astype(BF16)
        ybuf_ref[0:hr] = halo_ref[...] * keep
        ybuf_ref[hr:] = y_ref[...]
        acc_ref[...] = h_ref[...]

    yb = ybuf_ref[...]
    a_ref[:, 0:tf] = _dot(yb, wg_ref[...])
    a_ref[:, tf:] = _dot(yb, wu_ref[...])

    def conv(cols, cw_ref, cb_ref):
        c = cb_ref[...] + cw_ref[CONV_WIDTH - 1:CONV_WIDTH, :] * a_ref[pl.ds(hr, tm), cols]
        for k in range(CONV_WIDTH - 1):
            shift = CONV_WIDTH - 1 - k
            c = c + cw_ref[k:k + 1, :] * a_ref[pl.ds(hr - shift, tm), cols]
        return c

    cg = conv(slice(0, tf), cwg_ref, cbg_ref)
    cu = conv(slice(tf, 2 * tf), cwu_ref, cbu_ref)
    hmid = (cg * jax.nn.sigmoid(cg) * cu).astype(BF16)
    acc_ref[...] += _dot(hmid, wd_ref[...])

    @pl.when(j == pl.num_programs(1) - 1)
    def _():
        hh = acc_ref[...]
        ms = jnp.mean(hh * hh, axis=-1, keepdims=True)
        o_ref[...] = hh * lax.rsqrt(ms + EPS) * gf_ref[...]


def _ffn(y, h1, w_up, conv_w, conv_b, w_down, gf, seq):
    T, D = h1.shape
    dff = w_down.shape[0]
    tm, tf = TM_FFN, TF_FFN
    hr = V7X_BF16_SUBLANE_PACK
    nj = dff // tf
    halo_blocks = tm // hr
    in_specs = [
        pl.BlockSpec((tm, D), lambda i, j: (i, 0)),
        pl.BlockSpec((hr, D), lambda i, j: (jnp.maximum(i * halo_blocks - 1, 0), 0)),
        pl.BlockSpec((D, tf), lambda i, j: (0, j)),
        pl.BlockSpec((D, tf), lambda i, j: (0, nj + j)),
        pl.BlockSpec((CONV_WIDTH, tf), lambda i, j: (0, j)),
        pl.BlockSpec((CONV_WIDTH, tf), lambda i, j: (0, nj + j)),
        pl.BlockSpec((1, tf), lambda i, j: (0, j)),
        pl.BlockSpec((1, tf), lambda i, j: (0, nj + j)),
        pl.BlockSpec((tf, D), lambda i, j: (j, 0)),
        pl.BlockSpec((tm, D), lambda i, j: (i, 0)),
        _const_spec((1, D)),
    ]
    return pl.pallas_call(
        functools.partial(_ffn_kernel, tiles_per_seq=seq // tm),
        grid=(T // tm, nj), in_specs=in_specs,
        out_specs=pl.BlockSpec((tm, D), lambda i, j: (i, 0)),
        out_shape=jax.ShapeDtypeStruct((T, D), F32),
        scratch_shapes=[pltpu.VMEM((tm + hr, D), BF16),
                        pltpu.VMEM((tm + hr, 2 * tf), F32),
                        pltpu.VMEM((tm, D), F32)],
        compiler_params=_params(("arbitrary", "arbitrary")),
        name="ffn",
    )(y, y, w_up, w_up, conv_w, conv_w, conv_b, conv_b, w_down, h1, gf)


def _agg_t(n_sel, ncp, n_cmp):
    c_start = np.arange(ncp) * CMP_STRIDE
    js = np.arange(n_sel)[:, None] * SEL_LEN
    a = (c_start[None, :] < js + SEL_LEN) & (c_start[None, :] + CMP_LEN > js)
    a &= (np.arange(ncp) < n_cmp)[None, :]
    return jnp.asarray(a.astype(np.float32), dtype=BF16)


def _layer(h, positions, norm1_g, w_in, cmp_pe_k, cmp_w_k, cmp_pe_v, cmp_w_v, gmlp_norm_g,
           gmlp_w_s, gmlp_b_s, nsa_out_g, gmlp_out_g, w_out, norm2_g, w_up, conv_w, conv_b,
           w_down, out_g):
    B, S, D = h.shape
    T = B * S
    nq = NSA_HEADS * HEAD_DIM
    kvw = NSA_KV_GROUPS * HEAD_DIM
    x2 = h.reshape(T, D)
    pos2 = positions.reshape(T, 1).astype(jnp.int32)

    half = HEAD_DIM // 2
    inv = ROPE_THETA ** (-2.0 * jnp.arange(half, dtype=F32) / HEAD_DIM)
    inv_full = jnp.concatenate([inv, inv])[None, :]
    sign = jnp.asarray(np.concatenate([-np.ones(half), np.ones(half)])[None, :], dtype=F32)

    kv_w = w_in[:, nq:nq + 6 * kvw].reshape(D, 6, kvw)
    k_cmp_w, v_cmp_w, k_sel_w, v_sel_w, k_win_w, v_win_w = (kv_w[:, i] for i in range(6))
    g_off = nq + 6 * kvw
    n_gate = NSA_HEADS * N_BRANCH
    gate_w = w_in[:, g_off:g_off + n_gate].reshape(D, NSA_KV_GROUPS, NSA_REP * N_BRANCH)
    gate_w = jnp.pad(gate_w, ((0, 0), (0, 0), (0, GATE_ROWS - NSA_REP * N_BRANCH)))
    gate_w = gate_w.reshape(D, NSA_KV_GROUPS * GATE_ROWS)
    u_off = g_off + n_gate
    gw = GMLP_GROUPS * GMLP_GROUP_DIM
    u_w = w_in[:, u_off:u_off + gw]
    v_w = w_in[:, u_off + gw:u_off + 2 * gw]
    wn = jnp.concatenate([w_in[:, :nq], k_sel_w, k_win_w, k_cmp_w, v_cmp_w, u_w, v_w],
                         axis=1).astype(BF16)
    wt = jnp.concatenate([v_sel_w, v_win_w, gate_w], axis=1).T.astype(BF16)

    q, ksw, kvc, u, vg, vt, gt = _inproj(
        x2, pos2, norm1_g.reshape(1, D), inv_full, sign, wn, wt, gmlp_norm_g.reshape(1, gw))

    ncp = S // CMP_STRIDE
    n_cmp = (S - CMP_LEN) // CMP_STRIDE + 1
    hb = CMP_LEN // 2
    kvc2 = kvc.reshape(4, B * ncp, CMP_STRIDE * HEAD_DIM)

    def cmp_weights(w, pe):
        w2 = jnp.concatenate([w[:hb].reshape(hb * HEAD_DIM, HEAD_DIM),
                              w[hb:].reshape(hb * HEAD_DIM, HEAD_DIM)], axis=1).astype(BF16)
        pe8 = jnp.pad(pe.reshape(1, CMP_LEN * HEAD_DIM), ((0, 7), (0, 0))).astype(BF16)
        return w2, pe8, w.reshape(CMP_LEN * HEAD_DIM, HEAD_DIM).astype(BF16)

    posc = jnp.pad(positions[:, CMP_LEN - 1::CMP_STRIDE], ((0, 0), (0, ncp - n_cmp)))
    posc = posc.reshape(B * ncp, 1).astype(jnp.int32)
    kc = _compress(kvc2, *cmp_weights(cmp_w_k, cmp_pe_k), B, rope_args=(posc, inv_full, sign))
    vct = _compress(kvc2, *cmp_weights(cmp_w_v, cmp_pe_v), B)

    o_nsa = _nsa(q, kc, vct, ksw, vt, gt, _agg_t(S // SEL_LEN, ncp, n_cmp), B, S)

    h1, y = _mix_out(x2, o_nsa, u, vg, gmlp_w_s, gmlp_b_s.T, nsa_out_g.reshape(1, nq),
                     gmlp_out_g.reshape(1, gw), w_out.astype(BF16), norm2_g.reshape(1, D))

    out = _ffn(y, h1, w_up.astype(BF16), conv_w, conv_b.reshape(1, -1), w_down.astype(BF16),
               out_g.reshape(1, D), S)
    return out.reshape(B, S, D)


def kernel(x, positions, norm1_g, w_in, cmp_pe_k, cmp_w_k, cmp_pe_v, cmp_w_v, gmlp_norm_g,
           gmlp_w_s, gmlp_b_s, nsa_out_g, gmlp_out_g, w_out, norm2_g, w_up, conv_w, conv_b,
           w_down, final_g):
    depth = norm1_g.shape[0]
    assert depth == 1, "the FFN kernel fuses the final RMSNorm into the only layer"
    return _layer(x, positions, norm1_g[0], w_in[0], cmp_pe_k[0], cmp_w_k[0], cmp_pe_v[0],
                  cmp_w_v[0], gmlp_norm_g[0], gmlp_w_s[0], gmlp_b_s[0], nsa_out_g[0],
                  gmlp_out_g[0], w_out[0], norm2_g[0], w_up[0], conv_w[0], conv_b[0],
                  w_down[0], final_g)
```

```python
import functools

import numpy as np
import jax
import jax.numpy as jnp
from jax import lax
from jax.experimental import pallas as pl
from jax.experimental.pallas import tpu as pltpu

F32 = jnp.float32
BF16 = jnp.bfloat16

HEAD_DIM = 128
NSA_HEADS = 8
NSA_KV_GROUPS = 2
NSA_REP = NSA_HEADS // NSA_KV_GROUPS
N_BRANCH = 3
CMP_LEN = 32
CMP_STRIDE = 16
SEL_LEN = 64
SEL_TOPK = 16
WINDOW = 512
GMLP_GROUP_DIM = 128
GMLP_GROUPS = 8
GMLP_CHUNK = 128
CONV_WIDTH = 3
ROPE_THETA = 10000.0
EPS = 1e-6
Q_BLOCK = 128
NEG_INF = -1e30
LOG2_E = float(np.log2(np.e))
FIXED_SHIFT_LIMIT = 1e30
N_FORCED = 3
TOPK_FIRST_QB = SEL_TOPK * SEL_LEN // Q_BLOCK

V7X_LANES = 128
V7X_BF16_SUBLANE_PACK = 16
V7X_VMEM_LIMIT_BYTES = 56 * 1024 * 1024

TM_PROJ = 512
TM_MIX = 512
TM_FFN = 512
TF_FFN = 512
KV_CHUNK = 512
SEL_UNROLL = 8
ONLINE_UNROLL = 2
BIAS_SLOTS = 128
GATE_ROWS = 16


def _dot(a, b):
    return jnp.dot(a, b, preferred_element_type=F32)


def _dot_nt(a, b):
    return lax.dot_general(a, b, (((1,), (1,)), ((), ())), preferred_element_type=F32)


def _const_spec(shape):
    nd = len(shape)
    return pl.BlockSpec(shape, lambda *_: (0,) * nd, pipeline_mode=pl.Buffered(1))


def _params(semantics):
    return pltpu.CompilerParams(dimension_semantics=semantics,
                                vmem_limit_bytes=V7X_VMEM_LIMIT_BYTES)


def _gelu(x):
    return 0.5 * x * (1.0 + lax.erf(x * (2.0 ** -0.5)))


def _rope(x, cos, sin_signed):
    return x * cos + pltpu.roll(x, HEAD_DIM // 2, 1) * sin_signed


def _inproj_kernel(x_ref, pos_ref, g1_ref, inv_ref, sign_ref, wn_ref, wt_ref, gg_ref,
                   q_ref, ksw_ref, kvc_ref, u_ref, vg_ref, vt_ref, gt_ref):
    x = x_ref[...]
    ms = jnp.mean(x * x, axis=-1, keepdims=True)
    xn = (x * lax.rsqrt(ms + EPS) * g1_ref[...]).astype(BF16)

    ang = pos_ref[...].astype(F32) * inv_ref[...]
    cos = jnp.cos(ang)
    sin_s = jnp.sin(ang) * sign_ref[...]
    scale = HEAD_DIM ** -0.5

    seg_w = 4 * HEAD_DIM

    def seg(i):
        return _dot(xn, wn_ref[:, i * seg_w:(i + 1) * seg_w])

    for i in range(2):
        acc = seg(i)
        for h in range(4):
            qh = _rope(acc[:, h * HEAD_DIM:(h + 1) * HEAD_DIM], cos, sin_s) * scale
            c0 = (i * 4 + h) * HEAD_DIM
            q_ref[:, c0:c0 + HEAD_DIM] = qh.astype(BF16)
    acc = seg(2)
    for h in range(4):
        kh = _rope(acc[:, h * HEAD_DIM:(h + 1) * HEAD_DIM], cos, sin_s)
        ksw_ref[:, h * HEAD_DIM:(h + 1) * HEAD_DIM] = kh.astype(BF16)
    acc = seg(3)
    for s in range(4):
        kvc_ref[s] = acc[:, s * HEAD_DIM:(s + 1) * HEAD_DIM].astype(BF16)
    for i in range(2):
        acc = seg(4 + i)
        u_ref[:, i * seg_w:(i + 1) * seg_w] = _gelu(acc).astype(BF16)
    for i in range(2):
        acc = _gelu(seg(6 + i))
        for h in range(4):
            c0 = i * seg_w + h * GMLP_GROUP_DIM
            vh = acc[:, h * GMLP_GROUP_DIM:(h + 1) * GMLP_GROUP_DIM]
            msv = jnp.mean(vh * vh, axis=-1, keepdims=True)
            vn = vh * lax.rsqrt(msv + EPS) * gg_ref[:, c0:c0 + GMLP_GROUP_DIM]
            vg_ref[:, c0:c0 + GMLP_GROUP_DIM] = vn.astype(BF16)
    rt = _dot_nt(wt_ref[...], xn)
    nv = 4 * HEAD_DIM
    vt_ref[...] = rt[:nv].astype(BF16)
    gt_ref[...] = rt[nv:]


def _inproj(x2, pos2, g1, inv_full, sign, wn, wt, gg):
    T, D = x2.shape
    tm = TM_PROJ
    nt_rows = wt.shape[0]
    grid = (T // tm,)
    tok = lambda w: pl.BlockSpec((tm, w), lambda i: (i, 0))
    out_shape = (
        jax.ShapeDtypeStruct((T, NSA_HEADS * HEAD_DIM), BF16),
        jax.ShapeDtypeStruct((T, 4 * HEAD_DIM), BF16),
        jax.ShapeDtypeStruct((4, T, HEAD_DIM), BF16),
        jax.ShapeDtypeStruct((T, GMLP_GROUPS * GMLP_GROUP_DIM), BF16),
        jax.ShapeDtypeStruct((T, GMLP_GROUPS * GMLP_GROUP_DIM), BF16),
        jax.ShapeDtypeStruct((4 * HEAD_DIM, T), BF16),
        jax.ShapeDtypeStruct((NSA_KV_GROUPS * GATE_ROWS, T), F32),
    )
    out_specs = (
        tok(NSA_HEADS * HEAD_DIM),
        tok(4 * HEAD_DIM),
        pl.BlockSpec((4, tm, HEAD_DIM), lambda i: (0, i, 0)),
        tok(GMLP_GROUPS * GMLP_GROUP_DIM),
        tok(GMLP_GROUPS * GMLP_GROUP_DIM),
        pl.BlockSpec((4 * HEAD_DIM, tm), lambda i: (0, i)),
        pl.BlockSpec((NSA_KV_GROUPS * GATE_ROWS, tm), lambda i: (0, i)),
    )
    in_specs = [
        tok(D),
        pl.BlockSpec((tm, 1), lambda i: (i, 0)),
        _const_spec((1, D)),
        _const_spec((1, HEAD_DIM)),
        _const_spec((1, HEAD_DIM)),
        _const_spec(wn.shape),
        _const_spec((nt_rows, D)),
        _const_spec((1, GMLP_GROUPS * GMLP_GROUP_DIM)),
    ]
    return pl.pallas_call(
        _inproj_kernel, grid=grid, in_specs=in_specs, out_specs=out_specs,
        out_shape=out_shape, compiler_params=_params(("arbitrary",)),
        name="inproj",
    )(x2, pos2, g1, inv_full, sign, wn, wt, gg)


def _compress_body(x_ref, w2_ref, pe_ref, wflat_ref, shift_ref):
    ncp = x_ref.shape[0]
    p = _dot(x_ref[...], w2_ref[...])
    bias = _dot(pe_ref[...], wflat_ref[...])[0:1]
    shift_ref[0:ncp] = p[:, HEAD_DIM:]
    shift_ref[ncp:ncp + 8] = jnp.zeros((8, HEAD_DIM), F32)
    return p[:, :HEAD_DIM] + shift_ref[pl.ds(1, ncp), :] + bias


def _compress_k_kernel(x_ref, w2_ref, pe_ref, wflat_ref, posc_ref, inv_ref, sign_ref,
                       kc_ref, shift_ref):
    kc = _compress_body(x_ref, w2_ref, pe_ref, wflat_ref, shift_ref)
    ang = posc_ref[...].astype(F32) * inv_ref[...]
    kc_ref[...] = _rope(kc, jnp.cos(ang), jnp.sin(ang) * sign_ref[...]).astype(BF16)


def _compress_v_kernel(x_ref, w2_ref, pe_ref, wflat_ref, vct_ref, shift_ref):
    vc = _compress_body(x_ref, w2_ref, pe_ref, wflat_ref, shift_ref)
    vct_ref[...] = vc.T.astype(BF16)


def _compress(kvc, w2, pe8, wflat, batch, rope_args=None):
    ncp = kvc.shape[1] // batch
    kdim = kvc.shape[2]
    grid = (batch, NSA_KV_GROUPS)
    plane0 = 0 if rope_args is not None else NSA_KV_GROUPS
    in_specs = [
        pl.BlockSpec((None, ncp, kdim), lambda b, g: (plane0 + g, b, 0)),
        _const_spec(w2.shape),
        _const_spec(pe8.shape),
        _const_spec(wflat.shape),
    ]
    scratch = [pltpu.VMEM((ncp + 8, HEAD_DIM), F32)]
    if rope_args is not None:
        posc, inv_full, sign = rope_args
        in_specs += [pl.BlockSpec((ncp, 1), lambda b, g: (b, 0)),
                     _const_spec((1, HEAD_DIM)), _const_spec((1, HEAD_DIM))]
        return pl.pallas_call(
            _compress_k_kernel, grid=grid, in_specs=in_specs,
            out_specs=pl.BlockSpec((None, ncp, HEAD_DIM), lambda b, g: (g, b, 0)),
            out_shape=jax.ShapeDtypeStruct((NSA_KV_GROUPS, batch * ncp, HEAD_DIM), BF16),
            scratch_shapes=scratch, compiler_params=_params(("arbitrary", "arbitrary")),
            name="compress_k",
        )(kvc, w2, pe8, wflat, posc, inv_full, sign)
    return pl.pallas_call(
        _compress_v_kernel, grid=grid, in_specs=in_specs,
        out_specs=pl.BlockSpec((None, HEAD_DIM, ncp), lambda b, g: (g, 0, b)),
        out_shape=jax.ShapeDtypeStruct((NSA_KV_GROUPS, HEAD_DIM, batch * ncp), BF16),
        scratch_shapes=scratch, compiler_params=_params(("arbitrary", "arbitrary")),
        name="compress_v",
    )(kvc, w2, pe8, wflat)


def _nsa_kernel(q_ref, kc_ref, vct_ref, ksel_ref, kwin_ref, vselt_ref, vwint_ref, gt_ref,
                aggt_ref, eblk_ref, cbias_ref, wbias_ref, ones_ref, o_ref, qaug_ref, sc_ref):
    qb = pl.program_id(2)
    nq = NSA_REP * Q_BLOCK
    ncp = kc_ref.shape[0]
    nsel = aggt_ref.shape[0]
    n_var = qaug_ref.shape[0]
    chunks_per_var = eblk_ref.shape[0]

    qt_f32 = jnp.concatenate(
        [q_ref[:, r * HEAD_DIM:(r + 1) * HEAD_DIM].astype(F32).T for r in range(NSA_REP)], axis=1)
    qt = qt_f32.astype(BF16)
    qt2 = (qt_f32 * LOG2_E).astype(BF16)
    lane4 = lax.broadcasted_iota(jnp.int32, (1, nq), 1)
    t4 = qb * Q_BLOCK + (lane4 & (Q_BLOCK - 1))

    def pv_aug(vt, pr):
        vt_aug = jnp.concatenate([vt, ones_ref[:, 0:vt.shape[1]]], axis=0)
        return _dot(vt_aug, pr)

    def pv_and_rowsum(vt, pr):
        r_aug = pv_aug(vt, pr)
        return r_aug[0:HEAD_DIM], r_aug[HEAD_DIM:HEAD_DIM + 1]

    def compressed_window_select(nc, ns):
        s = _dot(kc_ref[0:nc, :], qt)
        wk = WINDOW + Q_BLOCK
        w0_ = pl.multiple_of(jnp.maximum(qb * Q_BLOCK - WINDOW, 0), Q_BLOCK)
        sw = _dot(kwin_ref[pl.ds(w0_, wk), :], qt2)

        cb0 = pl.multiple_of(ncp - qb * (Q_BLOCK // CMP_STRIDE), Q_BLOCK // CMP_STRIDE)
        s = s + jnp.concatenate([cbias_ref[pl.ds(cb0, nc), :]] * NSA_REP, axis=1)
        m = jnp.max(s, axis=0, keepdims=True)
        e = jnp.exp(s - m)
        oc, l = pv_and_rowsum(vct_ref[:, 0:nc], e.astype(BF16))
        inv_l = jnp.where(t4 >= CMP_LEN - 1, 1.0 / l, 0.0)
        o_c = oc * inv_l

        ps = e[:, 0:Q_BLOCK] * inv_l[:, 0:Q_BLOCK]
        for r in range(1, NSA_REP):
            cs = slice(r * Q_BLOCK, (r + 1) * Q_BLOCK)
            ps = ps + e[:, cs] * inv_l[:, cs]
        ps_hi = ps.astype(BF16)
        ps_lo = (ps - ps_hi.astype(F32)).astype(BF16)
        aggt = aggt_ref[0:ns, 0:nc]
        imp = _dot(aggt, ps_hi) + _dot(aggt, ps_lo)

        wv = jnp.minimum(qb, WINDOW // Q_BLOCK)
        sw = sw + jnp.concatenate([wbias_ref[wv]] * NSA_REP, axis=1)
        mw = jnp.max(sw, axis=0, keepdims=True)
        ew = jnp.exp2(sw - mw)
        ow, lw = pv_and_rowsum(vwint_ref[:, pl.ds(w0_, wk)], ew.astype(BF16))
        o_w = ow * (1.0 / lw)

        jj = lax.broadcasted_iota(jnp.int32, (ns, Q_BLOCK), 0)
        t1 = qb * Q_BLOCK + lax.broadcasted_iota(jnp.int32, (1, Q_BLOCK), 1)
        cur = t1 >> (SEL_LEN.bit_length() - 1)
        valid = jj <= cur

        forced = (jj == 0) | (jj == cur) | (jj == cur - 1)
        cand = valid & jnp.logical_not(forced)
        lowest = -1.0
        w = jnp.where(cand, imp, lowest)
        jf = jj.astype(F32)
        for _ in range(SEL_TOPK - N_FORCED):
            mx = jnp.max(w, axis=0, keepdims=True)
            idx = jnp.min(jnp.where(w == mx, jf, float(ns)), axis=0, keepdims=True)
            w = jnp.where(jf == idx, lowest, w)
        picked = forced | (cand & (w == lowest))
        sb = jnp.where(qb < TOPK_FIRST_QB,
                       jnp.where(valid, 0.0, NEG_INF), jnp.where(picked, 0.0, NEG_INF))
        if ns < nsel:
            sb = jnp.concatenate([sb, jnp.full((nsel - ns, Q_BLOCK), NEG_INF, F32)], axis=0)
        return o_c, sb, o_w

    if (ncp // 2) % V7X_LANES == 0 and (nsel // 2) % V7X_BF16_SUBLANE_PACK == 0:
        o_cmp, selb, o_win = lax.cond(qb < nsel // 4,
                                      lambda: compressed_window_select(ncp // 2, nsel // 2),
                                      lambda: compressed_window_select(ncp, nsel))
    else:
        o_cmp, selb, o_win = compressed_window_select(ncp, nsel)

    selb4 = jnp.concatenate([selb] * NSA_REP, axis=1)
    pad_rows = n_var * BIAS_SLOTS - nsel
    if pad_rows:
        selb4 = jnp.concatenate([selb4, jnp.zeros((pad_rows, nq), F32)], axis=0)
    for v in range(n_var):
        qaug_ref[v, 0:HEAD_DIM, :] = qt2
        qaug_ref[v, HEAD_DIM:, :] = selb4[v * BIAS_SLOTS:(v + 1) * BIAS_SLOTS].astype(BF16)

    k_idx = lax.broadcasted_iota(jnp.int32, (KV_CHUNK, 1), 0)

    def sel_scores(c):
        k0 = pl.multiple_of(c * KV_CHUNK, KV_CHUNK)
        k_aug = jnp.concatenate([ksel_ref[pl.ds(k0, KV_CHUNK), :], eblk_ref[c % chunks_per_var]],
                                axis=1)
        return _dot(k_aug, qaug_ref[c // chunks_per_var])

    def causal_mask(c, sc):
        return jnp.where((c * KV_CHUNK + k_idx) <= t4, sc, NEG_INF)

    def sel_pv(c, pc):
        k0 = pl.multiple_of(c * KV_CHUNK, KV_CHUNK)
        return pv_aug(vselt_ref[:, pl.ds(k0, KV_CHUNK)], pc.astype(BF16))

    def online_update(c, sc, carry, causal):
        m_i, acc = carry
        if causal:
            sc = causal_mask(c, sc)
        m_new = jnp.maximum(m_i, jnp.max(sc, axis=0, keepdims=True))
        alpha = jnp.exp2(m_i - m_new)
        return m_new, alpha * acc + sel_pv(c, jnp.exp2(sc - m_new))

    def sel_chunks(c0, carry, update, n, last):
        sc = sc_ref[...]
        for u in range(n):
            nxt = None if last else sel_scores(c0 + u + 1)
            carry = update(c0 + u, sc, carry, last)
            sc = nxt
        if not last:
            sc_ref[...] = sc
        return carry

    def sel_branch(update, init, sc_first, unroll):
        diag = qb // (KV_CHUNK // Q_BLOCK)
        n_groups = diag // unroll
        sc_ref[...] = sc_first
        carry = lax.fori_loop(
            0, n_groups, lambda gi, cy: sel_chunks(gi * unroll, cy, update, unroll, False), init)
        c = n_groups * unroll
        size = unroll // 2
        while size >= 1:
            take = ((diag - c) & size) != 0
            carry = lax.cond(
                take, lambda cy, c=c, size=size: sel_chunks(c, cy, update, size, False),
                lambda cy: cy, carry)
            c = c + jnp.where(take, size, 0)
            size //= 2
        return sel_chunks(diag, carry, update, 1, True)

    acc0 = jnp.zeros((HEAD_DIM + ones_ref.shape[0], nq), F32)

    sc0 = sel_scores(0)
    m0 = sc0[0:1, :]

    def fixed_shift_update(c, sc, acc, causal):
        if causal:
            sc = causal_mask(c, sc)
        return acc + sel_pv(c, jnp.exp2(sc - m0))

    acc_fast = sel_branch(fixed_shift_update, acc0, sc0, SEL_UNROLL)

    gate = jax.nn.sigmoid(gt_ref[...])

    def write_output(acc_sel):
        o_sel = acc_sel[0:HEAD_DIM] * (1.0 / acc_sel[HEAD_DIM:HEAD_DIM + 1])
        for r in range(NSA_REP):
            cs = slice(r * Q_BLOCK, (r + 1) * Q_BLOCK)
            g0 = gate[N_BRANCH * r + 0:N_BRANCH * r + 1, :]
            g1 = gate[N_BRANCH * r + 1:N_BRANCH * r + 2, :]
            g2 = gate[N_BRANCH * r + 2:N_BRANCH * r + 3, :]
            ot = g0 * o_cmp[:, cs] + g1 * o_sel[:, cs] + g2 * o_win[:, cs]
            o_ref[:, r * HEAD_DIM:(r + 1) * HEAD_DIM] = ot.T

    write_output(acc_fast)

    out_of_range = jnp.logical_not(jnp.max(jnp.abs(acc_fast)) < FIXED_SHIFT_LIMIT)

    @pl.when(out_of_range)
    def _():
        init = (jnp.full((1, nq), NEG_INF, F32), acc0)
        write_output(sel_branch(online_update, init, sel_scores(0), ONLINE_UNROLL)[1])


def _block_onehot(n_chunks):
    per_chunk = KV_CHUNK // SEL_LEN
    k = np.arange(KV_CHUNK)[None, :, None]
    e = np.arange(n_chunks)[:, None, None]
    x = np.arange(BIAS_SLOTS)[None, None, :]
    return jnp.asarray((x == e * per_chunk + k // SEL_LEN).astype(np.float32), dtype=BF16)


def _cmp_bias(ncp):
    n_rel = np.arange(2 * ncp)[:, None] - ncp
    q_rel = np.arange(Q_BLOCK)[None, :]
    vis = n_rel * CMP_STRIDE + (CMP_LEN - 1) <= q_rel
    return jnp.asarray(np.where(vis, 0.0, NEG_INF), dtype=F32)


def _win_bias():
    n_var = WINDOW // Q_BLOCK
    base = np.minimum(np.arange(n_var + 1) * Q_BLOCK, WINDOW)[:, None, None]
    d = base + np.arange(Q_BLOCK)[None, None, :] - np.arange(WINDOW + Q_BLOCK)[None, :, None]
    return jnp.asarray(np.where((d >= 0) & (d < WINDOW), 0.0, NEG_INF), dtype=F32)


def _nsa(q, kc, vct, ksw, vt, gt, aggt, batch, seq):
    T = q.shape[0]
    n_qb = seq // Q_BLOCK
    ncp = kc.shape[1] // batch
    nsel = aggt.shape[0]
    gw = NSA_REP * HEAD_DIM
    assert seq % KV_CHUNK == 0
    n_var = pl.cdiv(nsel, BIAS_SLOTS)
    eblk = _block_onehot(min(BIAS_SLOTS * SEL_LEN, seq) // KV_CHUNK)
    cbias = _cmp_bias(ncp)
    wbias = _win_bias()
    ones_w = max(ncp, WINDOW + Q_BLOCK, KV_CHUNK)
    ones_rows = jnp.asarray(np.arange(V7X_BF16_SUBLANE_PACK)[:, None] == 0, dtype=BF16)
    ones_rows = jnp.broadcast_to(ones_rows, (V7X_BF16_SUBLANE_PACK, ones_w))
    big = lambda shape, imap: pl.BlockSpec(shape, imap, pipeline_mode=pl.Buffered(1))
    in_specs = [
        pl.BlockSpec((Q_BLOCK, gw), lambda b, g, i: (b * n_qb + i, g)),
        big((None, ncp, HEAD_DIM), lambda b, g, i: (g, b, 0)),
        big((None, HEAD_DIM, ncp), lambda b, g, i: (g, 0, b)),
        big((seq, HEAD_DIM), lambda b, g, i: (b, g)),
        big((seq, HEAD_DIM), lambda b, g, i: (b, NSA_KV_GROUPS + g)),
        big((HEAD_DIM, seq), lambda b, g, i: (g, b)),
        big((HEAD_DIM, seq), lambda b, g, i: (NSA_KV_GROUPS + g, b)),
        pl.BlockSpec((GATE_ROWS, Q_BLOCK), lambda b, g, i: (g, b * n_qb + i)),
        _const_spec(aggt.shape),
        _const_spec(eblk.shape),
        _const_spec(cbias.shape),
        _const_spec(wbias.shape),
        _const_spec(ones_rows.shape),
    ]
    return pl.pallas_call(
        _nsa_kernel, grid=(batch, NSA_KV_GROUPS, n_qb), in_specs=in_specs,
        out_specs=pl.BlockSpec((Q_BLOCK, gw), lambda b, g, i: (b * n_qb + i, g)),
        out_shape=jax.ShapeDtypeStruct((T, NSA_HEADS * HEAD_DIM), F32),
        scratch_shapes=[pltpu.VMEM((n_var, HEAD_DIM + BIAS_SLOTS, NSA_REP * Q_BLOCK), BF16),
                        pltpu.VMEM((KV_CHUNK, NSA_REP * Q_BLOCK), F32)],
        compiler_params=_params(("arbitrary", "arbitrary", "arbitrary")),
        name="nsa",
    )(q, kc, vct, ksw, ksw, vt, vt, gt, aggt, eblk, cbias, wbias, ones_rows)


def _mix_out_kernel(x_ref, on_ref, u_ref, vg_ref, ws_ref, bst_ref, gn_ref, gm_ref, wo_ref,
                    g2_ref, h_ref, y_ref, og_ref):
    tm = x_ref.shape[0]
    row = lax.broadcasted_iota(jnp.int32, (GMLP_CHUNK, GMLP_CHUNK), 0)
    col = lax.broadcasted_iota(jnp.int32, (GMLP_CHUNK, GMLP_CHUNK), 1)
    tril = col <= row
    for h in range(GMLP_GROUPS):
        ws = jnp.where(tril, ws_ref[h], 0.0).astype(BF16)
        bcol = bst_ref[:, h:h + 1]
        cs = slice(h * GMLP_GROUP_DIM, (h + 1) * GMLP_GROUP_DIM)
        for n in range(tm // GMLP_CHUNK):
            rs = slice(n * GMLP_CHUNK, (n + 1) * GMLP_CHUNK)
            mixed = _dot(ws, vg_ref[rs, cs]) + bcol
            og_ref[rs, cs] = u_ref[rs, cs].astype(F32) * mixed

    def rms(v, g):
        return v * lax.rsqrt(jnp.mean(v * v, axis=-1, keepdims=True) + EPS) * g

    half = on_ref.shape[1]
    mix_n = rms(on_ref[...], gn_ref[...]).astype(BF16)
    mix_g = rms(og_ref[...], gm_ref[...]).astype(BF16)
    h1 = x_ref[...] + _dot(mix_n, wo_ref[0:half, :]) + _dot(mix_g, wo_ref[half:, :])
    h_ref[...] = h1
    y_ref[...] = rms(h1, g2_ref[...]).astype(BF16)


def _mix_out(x2, o_nsa, u, vg, ws, bst, gn, gm, wo, g2):
    T, D = x2.shape
    tm = TM_MIX
    half = o_nsa.shape[1]
    tok = lambda w: pl.BlockSpec((tm, w), lambda i: (i, 0))
    in_specs = [tok(D), tok(half), tok(half), tok(half),
                _const_spec(ws.shape), _const_spec(bst.shape),
                _const_spec((1, half)), _const_spec((1, half)),
                _const_spec(wo.shape), _const_spec((1, D))]
    return pl.pallas_call(
        _mix_out_kernel, grid=(T // tm,), in_specs=in_specs,
        out_specs=(tok(D), tok(D)),
        out_shape=(jax.ShapeDtypeStruct((T, D), F32), jax.ShapeDtypeStruct((T, D), BF16)),
        scratch_shapes=[pltpu.VMEM((tm, half), F32)],
        compiler_params=_params(("arbitrary",)),
        name="mix_out",
    )(x2, o_nsa, u, vg, ws, bst, gn, gm, wo, g2)


def _ffn_kernel(y_ref, halo_ref, wg_ref, wu_ref, cwg_ref, cwu_ref, cbg_ref, cbu_ref, wd_ref,
                h_ref, gf_ref, o_ref, ybuf_ref, a_ref, acc_ref, *, tiles_per_seq):
    i = pl.program_id(0)
    j = pl.program_id(1)
    tm = y_ref.shape[0]
    tf = wg_ref.shape[1]
    hr = halo_ref.shape[0]

    @pl.when(j == 0)
    def _():
        keep = jnp.where(i % tiles_per_seq == 0, 0.0, 1.0).astype(BF16)
        ybuf_ref[0:hr] = halo_ref[...] * keep
        ybuf_ref[hr:] = y_ref[...]
        acc_ref[...] = h_ref[...]

    yb = ybuf_ref[...]
    a_ref[:, 0:tf] = _dot(yb, wg_ref[...])
    a_ref[:, tf:] = _dot(yb, wu_ref[...])

    def conv(cols, cw_ref, cb_ref):
        c = cb_ref[...] + cw_ref[CONV_WIDTH - 1:CONV_WIDTH, :] * a_ref[pl.ds(hr, tm), cols]
        for k in range(CONV_WIDTH - 1):
            shift = CONV_WIDTH - 1 - k
            c = c + cw_ref[k:k + 1, :] * a_ref[pl.ds(hr - shift, tm), cols]
        return c

    cg = conv(slice(0, tf), cwg_ref, cbg_ref)
    cu = conv(slice(tf, 2 * tf), cwu_ref, cbu_ref)
    hmid = (cg * jax.nn.sigmoid(cg) * cu).astype(BF16)
    acc_ref[...] += _dot(hmid, wd_ref[...])

    @pl.when(j == pl.num_programs(1) - 1)
    def _():
        hh = acc_ref[...]
        ms = jnp.mean(hh * hh, axis=-1, keepdims=True)
        o_ref[...] = hh * lax.rsqrt(ms + EPS) * gf_ref[...]


def _ffn(y, h1, w_up, conv_w, conv_b, w_down, gf, seq):
    T, D = h1.shape
    dff = w_down.shape[0]
    tm, tf = TM_FFN, TF_FFN
    hr = V7X_BF16_SUBLANE_PACK
    nj = dff // tf
    halo_blocks = tm // hr
    in_specs = [
        pl.BlockSpec((tm, D), lambda i, j: (i, 0)),
        pl.BlockSpec((hr, D), lambda i, j: (jnp.maximum(i * halo_blocks - 1, 0), 0)),
        pl.BlockSpec((D, tf), lambda i, j: (0, j)),
        pl.BlockSpec((D, tf), lambda i, j: (0, nj + j)),
        pl.BlockSpec((CONV_WIDTH, tf), lambda i, j: (0, j)),
        pl.BlockSpec((CONV_WIDTH, tf), lambda i, j: (0, nj + j)),
        pl.BlockSpec((1, tf), lambda i, j: (0, j)),
        pl.BlockSpec((1, tf), lambda i, j: (0, nj + j)),
        pl.BlockSpec((tf, D), lambda i, j: (j, 0)),
        pl.BlockSpec((tm, D), lambda i, j: (i, 0)),
        _const_spec((1, D)),
    ]
    return pl.pallas_call(
        functools.partial(_ffn_kernel, tiles_per_seq=seq // tm),
        grid=(T // tm, nj), in_specs=in_specs,
        out_specs=pl.BlockSpec((tm, D), lambda i, j: (i, 0)),
        out_shape=jax.ShapeDtypeStruct((T, D), F32),
        scratch_shapes=[pltpu.VMEM((tm + hr, D), BF16),
                        pltpu.VMEM((tm + hr, 2 * tf), F32),
                        pltpu.VMEM((tm, D), F32)],
        compiler_params=_params(("arbitrary", "arbitrary")),
        name="ffn",
    )(y, y, w_up, w_up, conv_w, conv_w, conv_b, conv_b, w_down, h1, gf)


def _agg_t(n_sel, ncp, n_cmp):
    c_start = np.arange(ncp) * CMP_STRIDE
    js = np.arange(n_sel)[:, None] * SEL_LEN
    a = (c_start[None, :] < js + SEL_LEN) & (c_start[None, :] + CMP_LEN > js)
    a &= (np.arange(ncp) < n_cmp)[None, :]
    return jnp.asarray(a.astype(np.float32), dtype=BF16)


def _layer(h, positions, norm1_g, w_in, cmp_pe_k, cmp_w_k, cmp_pe_v, cmp_w_v, gmlp_norm_g,
           gmlp_w_s, gmlp_b_s, nsa_out_g, gmlp_out_g, w_out, norm2_g, w_up, conv_w, conv_b,
           w_down, out_g):
    B, S, D = h.shape
    T = B * S
    nq = NSA_HEADS * HEAD_DIM
    kvw = NSA_KV_GROUPS * HEAD_DIM
    x2 = h.reshape(T, D)
    pos2 = positions.reshape(T, 1).astype(jnp.int32)

    half = HEAD_DIM // 2
    inv = ROPE_THETA ** (-2.0 * jnp.arange(half, dtype=F32) / HEAD_DIM)
    inv_full = jnp.concatenate([inv, inv])[None, :]
    sign = jnp.asarray(np.concatenate([-np.ones(half), np.ones(half)])[None, :], dtype=F32)

    kv_w = w_in[:, nq:nq + 6 * kvw].reshape(D, 6, kvw)
    k_cmp_w, v_cmp_w, k_sel_w, v_sel_w, k_win_w, v_win_w = (kv_w[:, i] for i in range(6))
    g_off = nq + 6 * kvw
    n_gate = NSA_HEADS * N_BRANCH
    gate_w = w_in[:, g_off:g_off + n_gate].reshape(D, NSA_KV_GROUPS, NSA_REP * N_BRANCH)
    gate_w = jnp.pad(gate_w, ((0, 0), (0, 0), (0, GATE_ROWS - NSA_REP * N_BRANCH)))
    gate_w = gate_w.reshape(D, NSA_KV_GROUPS * GATE_ROWS)
    u_off = g_off + n_gate
    gw = GMLP_GROUPS * GMLP_GROUP_DIM
    u_w = w_in[:, u_off:u_off + gw]
    v_w = w_in[:, u_off + gw:u_off + 2 * gw]
    wn = jnp.concatenate([w_in[:, :nq], k_sel_w, k_win_w, k_cmp_w, v_cmp_w, u_w, v_w],
                         axis=1).astype(BF16)
    wt = jnp.concatenate([v_sel_w, v_win_w, gate_w], axis=1).T.astype(BF16)

    q, ksw, kvc, u, vg, vt, gt = _inproj(
        x2, pos2, norm1_g.reshape(1, D), inv_full, sign, wn, wt, gmlp_norm_g.reshape(1, gw))

    ncp = S // CMP_STRIDE
    n_cmp = (S - CMP_LEN) // CMP_STRIDE + 1
    hb = CMP_LEN // 2
    kvc2 = kvc.reshape(4, B * ncp, CMP_STRIDE * HEAD_DIM)

    def cmp_weights(w, pe):
        w2 = jnp.concatenate([w[:hb].reshape(hb * HEAD_DIM, HEAD_DIM),
                              w[hb:].reshape(hb * HEAD_DIM, HEAD_DIM)], axis=1).astype(BF16)
        pe8 = jnp.pad(pe.reshape(1, CMP_LEN * HEAD_DIM), ((0, 7), (0, 0))).astype(BF16)
        return w2, pe8, w.reshape(CMP_LEN * HEAD_DIM, HEAD_DIM).astype(BF16)

    posc = jnp.pad(positions[:, CMP_LEN - 1::CMP_STRIDE], ((0, 0), (0, ncp - n_cmp)))
    posc = posc.reshape(B * ncp, 1).astype(jnp.int32)
    kc = _compress(kvc2, *cmp_weights(cmp_w_k, cmp_pe_k), B, rope_args=(posc, inv_full, sign))
    vct = _compress(kvc2, *cmp_weights(cmp_w_v, cmp_pe_v), B)

    o_nsa = _nsa(q, kc, vct, ksw, vt, gt, _agg_t(S // SEL_LEN, ncp, n_cmp), B, S)

    h1, y = _mix_out(x2, o_nsa, u, vg, gmlp_w_s, gmlp_b_s.T, nsa_out_g.reshape(1, nq),
                     gmlp_out_g.reshape(1, gw), w_out.astype(BF16), norm2_g.reshape(1, D))

    out = _ffn(y, h1, w_up.astype(BF16), conv_w, conv_b.reshape(1, -1), w_down.astype(BF16),
               out_g.reshape(1, D), S)
    return out.reshape(B, S, D)


def kernel(x, positions, norm1_g, w_in, cmp_pe_k, cmp_w_k, cmp_pe_v, cmp_w_v, gmlp_norm_g,
           gmlp_w_s, gmlp_b_s, nsa_out_g, gmlp_out_g, w_out, norm2_g, w_up, conv_w, conv_b,
           w_down, final_g):
    depth = norm1_g.shape[0]
    assert depth == 1, "the FFN kernel fuses the final RMSNorm into the only layer"
    return _layer(x, positions, norm1_g[0], w_in[0], cmp_pe_k[0], cmp_w_k[0], cmp_pe_v[0],
                  cmp_w_v[0], gmlp_norm_g[0], gmlp_w_s[0], gmlp_b_s[0], nsa_out_g[0],
                  gmlp_out_g[0], w_out[0], norm2_g[0], w_up[0], conv_w[0], conv_b[0],
                  w_down[0], final_g)
```

```python
import functools

import numpy as np
import jax
import jax.numpy as jnp
from jax import lax
from jax.experimental import pallas as pl
from jax.experimental.pallas import tpu as pltpu

F32 = jnp.float32
BF16 = jnp.bfloat16

HEAD_DIM = 128
NSA_HEADS = 8
NSA_KV_GROUPS = 2
NSA_REP = NSA_HEADS // NSA_KV_GROUPS
N_BRANCH = 3
CMP_LEN = 32
CMP_STRIDE = 16
SEL_LEN = 64
SEL_TOPK = 16
WINDOW = 512
GMLP_GROUP_DIM = 128
GMLP_GROUPS = 8
GMLP_CHUNK = 128
CONV_WIDTH = 3
ROPE_THETA = 10000.0
EPS = 1e-6
Q_BLOCK = 128
NEG_INF = -1e30
LOG2_E = float(np.log2(np.e))
FIXED_SHIFT_LIMIT = 1e30
N_FORCED = 3
TOPK_FIRST_QB = SEL_TOPK * SEL_LEN // Q_BLOCK

V7X_LANES = 128
V7X_BF16_SUBLANE_PACK = 16
V7X_VMEM_LIMIT_BYTES = 56 * 1024 * 1024

TM_PROJ = 512
TM_MIX = 512
TM_FFN = 512
TF_FFN = 512
KV_CHUNK = 512
SEL_UNROLL = 8
ONLINE_UNROLL = 2
BIAS_SLOTS = 128
GATE_ROWS = 16


def _dot(a, b):
    return jnp.dot(a, b, preferred_element_type=F32)


def _dot_nt(a, b):
    return lax.dot_general(a, b, (((1,), (1,)), ((), ())), preferred_element_type=F32)


def _const_spec(shape):
    nd = len(shape)
    return pl.BlockSpec(shape, lambda *_: (0,) * nd, pipeline_mode=pl.Buffered(1))


def _params(semantics):
    return pltpu.CompilerParams(dimension_semantics=semantics,
                                vmem_limit_bytes=V7X_VMEM_LIMIT_BYTES)


def _gelu(x):
    return 0.5 * x * (1.0 + lax.erf(x * (2.0 ** -0.5)))


def _rope(x, cos, sin_signed):
    return x * cos + pltpu.roll(x, HEAD_DIM // 2, 1) * sin_signed


def _inproj_kernel(x_ref, pos_ref, g1_ref, inv_ref, sign_ref, wn_ref, wt_ref, gg_ref,
                   q_ref, ksw_ref, kvc_ref, u_ref, vg_ref, vt_ref, gt_ref):
    x = x_ref[...]
    ms = jnp.mean(x * x, axis=-1, keepdims=True)
    xn = (x * lax.rsqrt(ms + EPS) * g1_ref[...]).astype(BF16)

    ang = pos_ref[...].astype(F32) * inv_ref[...]
    cos = jnp.cos(ang)
    sin_s = jnp.sin(ang) * sign_ref[...]
    scale = HEAD_DIM ** -0.5

    seg_w = 4 * HEAD_DIM

    def seg(i):
        return _dot(xn, wn_ref[:, i * seg_w:(i + 1) * seg_w])

    for i in range(2):
        acc = seg(i)
        for h in range(4):
            qh = _rope(acc[:, h * HEAD_DIM:(h + 1) * HEAD_DIM], cos, sin_s) * scale
            c0 = (i * 4 + h) * HEAD_DIM
            q_ref[:, c0:c0 + HEAD_DIM] = qh.astype(BF16)
    acc = seg(2)
    for h in range(4):
        kh = _rope(acc[:, h * HEAD_DIM:(h + 1) * HEAD_DIM], cos, sin_s)
        ksw_ref[:, h * HEAD_DIM:(h + 1) * HEAD_DIM] = kh.astype(BF16)
    acc = seg(3)
    for s in range(4):
        kvc_ref[s] = acc[:, s * HEAD_DIM:(s + 1) * HEAD_DIM].astype(BF16)
    for i in range(2):
        acc = seg(4 + i)
        u_ref[:, i * seg_w:(i + 1) * seg_w] = _gelu(acc).astype(BF16)
    for i in range(2):
        acc = _gelu(seg(6 + i))
        for h in range(4):
            c0 = i * seg_w + h * GMLP_GROUP_DIM
            vh = acc[:, h * GMLP_GROUP_DIM:(h + 1) * GMLP_GROUP_DIM]
            msv = jnp.mean(vh * vh, axis=-1, keepdims=True)
            vn = vh * lax.rsqrt(msv + EPS) * gg_ref[:, c0:c0 + GMLP_GROUP_DIM]
            vg_ref[:, c0:c0 + GMLP_GROUP_DIM] = vn.astype(BF16)
    rt = _dot_nt(wt_ref[...], xn)
    nv = 4 * HEAD_DIM
    vt_ref[...] = rt[:nv].astype(BF16)
    gt_ref[...] = rt[nv:]


def _inproj(x2, pos2, g1, inv_full, sign, wn, wt, gg):
    T, D = x2.shape
    tm = TM_PROJ
    nt_rows = wt.shape[0]
    grid = (T // tm,)
    tok = lambda w: pl.BlockSpec((tm, w), lambda i: (i, 0))
    out_shape = (
        jax.ShapeDtypeStruct((T, NSA_HEADS * HEAD_DIM), BF16),
        jax.ShapeDtypeStruct((T, 4 * HEAD_DIM), BF16),
        jax.ShapeDtypeStruct((4, T, HEAD_DIM), BF16),
        jax.ShapeDtypeStruct((T, GMLP_GROUPS * GMLP_GROUP_DIM), BF16),
        jax.ShapeDtypeStruct((T, GMLP_GROUPS * GMLP_GROUP_DIM), BF16),
        jax.ShapeDtypeStruct((4 * HEAD_DIM, T), BF16),
        jax.ShapeDtypeStruct((NSA_KV_GROUPS * GATE_ROWS, T), F32),
    )
    out_specs = (
        tok(NSA_HEADS * HEAD_DIM),
        tok(4 * HEAD_DIM),
        pl.BlockSpec((4, tm, HEAD_DIM), lambda i: (0, i, 0)),
        tok(GMLP_GROUPS * GMLP_GROUP_DIM),
        tok(GMLP_GROUPS * GMLP_GROUP_DIM),
        pl.BlockSpec((4 * HEAD_DIM, tm), lambda i: (0, i)),
        pl.BlockSpec((NSA_KV_GROUPS * GATE_ROWS, tm), lambda i: (0, i)),
    )
    in_specs = [
        tok(D),
        pl.BlockSpec((tm, 1), lambda i: (i, 0)),
        _const_spec((1, D)),
        _const_spec((1, HEAD_DIM)),
        _const_spec((1, HEAD_DIM)),
        _const_spec(wn.shape),
        _const_spec((nt_rows, D)),
        _const_spec((1, GMLP_GROUPS * GMLP_GROUP_DIM)),
    ]
    return pl.pallas_call(
        _inproj_kernel, grid=grid, in_specs=in_specs, out_specs=out_specs,
        out_shape=out_shape, compiler_params=_params(("arbitrary",)),
        name="inproj",
    )(x2, pos2, g1, inv_full, sign, wn, wt, gg)


def _compress_body(x_ref, w2_ref, pe_ref, wflat_ref, shift_ref):
    ncp = x_ref.shape[0]
    p = _dot(x_ref[...], w2_ref[...])
    bias = _dot(pe_ref[...], wflat_ref[...])[0:1]
    shift_ref[0:ncp] = p[:, HEAD_DIM:]
    shift_ref[ncp:ncp + 8] = jnp.zeros((8, HEAD_DIM), F32)
    return p[:, :HEAD_DIM] + shift_ref[pl.ds(1, ncp), :] + bias


def _compress_k_kernel(x_ref, w2_ref, pe_ref, wflat_ref, posc_ref, inv_ref, sign_ref,
                       kc_ref, shift_ref):
    kc = _compress_body(x_ref, w2_ref, pe_ref, wflat_ref, shift_ref)
    ang = posc_ref[...].astype(F32) * inv_ref[...]
    kc_ref[...] = _rope(kc, jnp.cos(ang), jnp.sin(ang) * sign_ref[...]).astype(BF16)


def _compress_v_kernel(x_ref, w2_ref, pe_ref, wflat_ref, vct_ref, shift_ref):
    vc = _compress_body(x_ref, w2_ref, pe_ref, wflat_ref, shift_ref)
    vct_ref[...] = vc.T.astype(BF16)


def _compress(kvc, w2, pe8, wflat, batch, rope_args=None):
    ncp = kvc.shape[1] // batch
    kdim = kvc.shape[2]
    grid = (batch, NSA_KV_GROUPS)
    plane0 = 0 if rope_args is not None else NSA_KV_GROUPS
    in_specs = [
        pl.BlockSpec((None, ncp, kdim), lambda b, g: (plane0 + g, b, 0)),
        _const_spec(w2.shape),
        _const_spec(pe8.shape),
        _const_spec(wflat.shape),
    ]
    scratch = [pltpu.VMEM((ncp + 8, HEAD_DIM), F32)]
    if rope_args is not None:
        posc, inv_full, sign = rope_args
        in_specs += [pl.BlockSpec((ncp, 1), lambda b, g: (b, 0)),
                     _const_spec((1, HEAD_DIM)), _const_spec((1, HEAD_DIM))]
        return pl.pallas_call(
            _compress_k_kernel, grid=grid, in_specs=in_specs,
            out_specs=pl.BlockSpec((None, ncp, HEAD_DIM), lambda b, g: (g, b, 0)),
            out_shape=jax.ShapeDtypeStruct((NSA_KV_GROUPS, batch * ncp, HEAD_DIM), BF16),
            scratch_shapes=scratch, compiler_params=_params(("arbitrary", "arbitrary")),
            name="compress_k",
        )(kvc, w2, pe8, wflat, posc, inv_full, sign)
    return pl.pallas_call(
        _compress_v_kernel, grid=grid, in_specs=in_specs,
        out_specs=pl.BlockSpec((None, HEAD_DIM, ncp), lambda b, g: (g, 0, b)),
        out_shape=jax.ShapeDtypeStruct((NSA_KV_GROUPS, HEAD_DIM, batch * ncp), BF16),
        scratch_shapes=scratch, compiler_params=_params(("arbitrary", "arbitrary")),
        name="compress_v",
    )(kvc, w2, pe8, wflat)


def _nsa_kernel(q_ref, kc_ref, vct_ref, ksel_ref, kwin_ref, vselt_ref, vwint_ref, gt_ref,
                aggt_ref, eblk_ref, cbias_ref, wbias_ref, ones_ref, o_ref, qaug_ref, sc_ref):
    qb = pl.program_id(2)
    nq = NSA_REP * Q_BLOCK
    ncp = kc_ref.shape[0]
    nsel = aggt_ref.shape[0]
    n_var = qaug_ref.shape[0]
    chunks_per_var = eblk_ref.shape[0]

    qt_f32 = jnp.concatenate(
        [q_ref[:, r * HEAD_DIM:(r + 1) * HEAD_DIM].astype(F32).T for r in range(NSA_REP)], axis=1)
    qt = qt_f32.astype(BF16)
    qt2 = (qt_f32 * LOG2_E).astype(BF16)
    lane4 = lax.broadcasted_iota(jnp.int32, (1, nq), 1)
    t4 = qb * Q_BLOCK + (lane4 & (Q_BLOCK - 1))

    def pv_aug(vt, pr):
        vt_aug = jnp.concatenate([vt, ones_ref[:, 0:vt.shape[1]]], axis=0)
        return _dot(vt_aug, pr)

    def pv_and_rowsum(vt, pr):
        r_aug = pv_aug(vt, pr)
        return r_aug[0:HEAD_DIM], r_aug[HEAD_DIM:HEAD_DIM + 1]

    def compressed_window_select(nc, ns):
        s = _dot(kc_ref[0:nc, :], qt)
        wk = WINDOW + Q_BLOCK
        w0_ = pl.multiple_of(jnp.maximum(qb * Q_BLOCK - WINDOW, 0), Q_BLOCK)
        sw = _dot(kwin_ref[pl.ds(w0_, wk), :], qt2)

        cb0 = pl.multiple_of(ncp - qb * (Q_BLOCK // CMP_STRIDE), Q_BLOCK // CMP_STRIDE)
        s = s + jnp.concatenate([cbias_ref[pl.ds(cb0, nc), :]] * NSA_REP, axis=1)
        m = jnp.max(s, axis=0, keepdims=True)
        e = jnp.exp(s - m)
        oc, l = pv_and_rowsum(vct_ref[:, 0:nc], e.astype(BF16))
        inv_l = jnp.where(t4 >= CMP_LEN - 1, 1.0 / l, 0.0)
        o_c = oc * inv_l

        ps = e[:, 0:Q_BLOCK] * inv_l[:, 0:Q_BLOCK]
        for r in range(1, NSA_REP):
            cs = slice(r * Q_BLOCK, (r + 1) * Q_BLOCK)
            ps = ps + e[:, cs] * inv_l[:, cs]
        ps_hi = ps.astype(BF16)
        ps_lo = (ps - ps_hi.astype(F32)).astype(BF16)
        aggt = aggt_ref[0:ns, 0:nc]
        imp = _dot(aggt, ps_hi) + _dot(aggt, ps_lo)

        wv = jnp.minimum(qb, WINDOW // Q_BLOCK)
        sw = sw + jnp.concatenate([wbias_ref[wv]] * NSA_REP, axis=1)
        mw = jnp.max(sw, axis=0, keepdims=True)
        ew = jnp.exp2(sw - mw)
        ow, lw = pv_and_rowsum(vwint_ref[:, pl.ds(w0_, wk)], ew.astype(BF16))
        o_w = ow * (1.0 / lw)

        jj = lax.broadcasted_iota(jnp.int32, (ns, Q_BLOCK), 0)
        t1 = qb * Q_BLOCK + lax.broadcasted_iota(jnp.int32, (1, Q_BLOCK), 1)
        cur = t1 >> (SEL_LEN.bit_length() - 1)
        valid = jj <= cur

        forced = (jj == 0) | (jj == cur) | (jj == cur - 1)
        cand = valid & jnp.logical_not(forced)
        lowest = -1.0
        w = jnp.where(cand, imp, lowest)
        jf = jj.astype(F32)
        for _ in range(SEL_TOPK - N_FORCED):
            mx = jnp.max(w, axis=0, keepdims=True)
            idx = jnp.min(jnp.where(w == mx, jf, float(ns)), axis=0, keepdims=True)
            w = jnp.where(jf == idx, lowest, w)
        picked = forced | (cand & (w == lowest))
        sb = jnp.where(qb < TOPK_FIRST_QB,
                       jnp.where(valid, 0.0, NEG_INF), jnp.where(picked, 0.0, NEG_INF))
        if ns < nsel:
            sb = jnp.concatenate([sb, jnp.full((nsel - ns, Q_BLOCK), NEG_INF, F32)], axis=0)
        return o_c, sb, o_w

    def sized_variant(div):
        aligned = (ncp // div) % V7X_LANES == 0 and (nsel // div) % V7X_BF16_SUBLANE_PACK == 0
        return (lambda: compressed_window_select(ncp // div, nsel // div)) if aligned else None

    full, half, quarter = sized_variant(1), sized_variant(2), sized_variant(4)
    n_qb = nsel * SEL_LEN // Q_BLOCK
    if half is None:
        o_cmp, selb, o_win = full()
    elif quarter is None:
        o_cmp, selb, o_win = lax.cond(qb < n_qb // 2, half, full)
    else:
        o_cmp, selb, o_win = lax.cond(
            qb < n_qb // 2, lambda: lax.cond(qb < n_qb // 4, quarter, half), full)

    selb4 = jnp.concatenate([selb] * NSA_REP, axis=1)
    pad_rows = n_var * BIAS_SLOTS - nsel
    if pad_rows:
        selb4 = jnp.concatenate([selb4, jnp.zeros((pad_rows, nq), F32)], axis=0)
    for v in range(n_var):
        qaug_ref[v, 0:HEAD_DIM, :] = qt2
        qaug_ref[v, HEAD_DIM:, :] = selb4[v * BIAS_SLOTS:(v + 1) * BIAS_SLOTS].astype(BF16)

    k_idx = lax.broadcasted_iota(jnp.int32, (KV_CHUNK, 1), 0)

    def sel_scores(c):
        k0 = pl.multiple_of(c * KV_CHUNK, KV_CHUNK)
        k_aug = jnp.concatenate([ksel_ref[pl.ds(k0, KV_CHUNK), :], eblk_ref[c % chunks_per_var]],
                                axis=1)
        return _dot(k_aug, qaug_ref[c // chunks_per_var])

    def causal_mask(c, sc):
        return jnp.where((c * KV_CHUNK + k_idx) <= t4, sc, NEG_INF)

    def sel_pv(c, pc):
        k0 = pl.multiple_of(c * KV_CHUNK, KV_CHUNK)
        return pv_aug(vselt_ref[:, pl.ds(k0, KV_CHUNK)], pc.astype(BF16))

    def online_update(c, sc, carry, causal):
        m_i, acc = carry
        if causal:
            sc = causal_mask(c, sc)
        m_new = jnp.maximum(m_i, jnp.max(sc, axis=0, keepdims=True))
        alpha = jnp.exp2(m_i - m_new)
        return m_new, alpha * acc + sel_pv(c, jnp.exp2(sc - m_new))

    def sel_chunks(c0, carry, update, n, last):
        sc = sc_ref[...]
        for u in range(n):
            nxt = None if last else sel_scores(c0 + u + 1)
            carry = update(c0 + u, sc, carry, last)
            sc = nxt
        if not last:
            sc_ref[...] = sc
        return carry

    def sel_branch(update, init, sc_first, unroll):
        diag = qb // (KV_CHUNK // Q_BLOCK)
        n_groups = diag // unroll
        sc_ref[...] = sc_first
        carry = lax.fori_loop(
            0, n_groups, lambda gi, cy: sel_chunks(gi * unroll, cy, update, unroll, False), init)
        c = n_groups * unroll
        size = unroll // 2
        while size >= 1:
            take = ((diag - c) & size) != 0
            carry = lax.cond(
                take, lambda cy, c=c, size=size: sel_chunks(c, cy, update, size, False),
                lambda cy: cy, carry)
            c = c + jnp.where(take, size, 0)
            size //= 2
        return sel_chunks(diag, carry, update, 1, True)

    acc0 = jnp.zeros((HEAD_DIM + ones_ref.shape[0], nq), F32)

    sc0 = sel_scores(0)
    m0 = sc0[0:1, :]

    def fixed_shift_update(c, sc, acc, causal):
        if causal:
            sc = causal_mask(c, sc)
        return acc + sel_pv(c, jnp.exp2(sc - m0))

    acc_fast = sel_branch(fixed_shift_update, acc0, sc0, SEL_UNROLL)

    gate = jax.nn.sigmoid(gt_ref[...])

    def write_output(acc_sel):
        o_sel = acc_sel[0:HEAD_DIM] * (1.0 / acc_sel[HEAD_DIM:HEAD_DIM + 1])
        for r in range(NSA_REP):
            cs = slice(r * Q_BLOCK, (r + 1) * Q_BLOCK)
            g0 = gate[N_BRANCH * r + 0:N_BRANCH * r + 1, :]
            g1 = gate[N_BRANCH * r + 1:N_BRANCH * r + 2, :]
            g2 = gate[N_BRANCH * r + 2:N_BRANCH * r + 3, :]
            ot = g0 * o_cmp[:, cs] + g1 * o_sel[:, cs] + g2 * o_win[:, cs]
            o_ref[:, r * HEAD_DIM:(r + 1) * HEAD_DIM] = ot.T

    write_output(acc_fast)

    out_of_range = jnp.logical_not(jnp.max(jnp.abs(acc_fast)) < FIXED_SHIFT_LIMIT)

    @pl.when(out_of_range)
    def _():
        init = (jnp.full((1, nq), NEG_INF, F32), acc0)
        write_output(sel_branch(online_update, init, sel_scores(0), ONLINE_UNROLL)[1])


def _block_onehot(n_chunks):
    per_chunk = KV_CHUNK // SEL_LEN
    k = np.arange(KV_CHUNK)[None, :, None]
    e = np.arange(n_chunks)[:, None, None]
    x = np.arange(BIAS_SLOTS)[None, None, :]
    return jnp.asarray((x == e * per_chunk + k // SEL_LEN).astype(np.float32), dtype=BF16)


def _cmp_bias(ncp):
    n_rel = np.arange(2 * ncp)[:, None] - ncp
    q_rel = np.arange(Q_BLOCK)[None, :]
    vis = n_rel * CMP_STRIDE + (CMP_LEN - 1) <= q_rel
    return jnp.asarray(np.where(vis, 0.0, NEG_INF), dtype=F32)


def _win_bias():
    n_var = WINDOW // Q_BLOCK
    base = np.minimum(np.arange(n_var + 1) * Q_BLOCK, WINDOW)[:, None, None]
    d = base + np.arange(Q_BLOCK)[None, None, :] - np.arange(WINDOW + Q_BLOCK)[None, :, None]
    return jnp.asarray(np.where((d >= 0) & (d < WINDOW), 0.0, NEG_INF), dtype=F32)


def _nsa(q, kc, vct, ksw, vt, gt, aggt, batch, seq):
    T = q.shape[0]
    n_qb = seq // Q_BLOCK
    ncp = kc.shape[1] // batch
    nsel = aggt.shape[0]
    gw = NSA_REP * HEAD_DIM
    assert seq % KV_CHUNK == 0
    n_var = pl.cdiv(nsel, BIAS_SLOTS)
    eblk = _block_onehot(min(BIAS_SLOTS * SEL_LEN, seq) // KV_CHUNK)
    cbias = _cmp_bias(ncp)
    wbias = _win_bias()
    ones_w = max(ncp, WINDOW + Q_BLOCK, KV_CHUNK)
    ones_rows = jnp.asarray(np.arange(V7X_BF16_SUBLANE_PACK)[:, None] == 0, dtype=BF16)
    ones_rows = jnp.broadcast_to(ones_rows, (V7X_BF16_SUBLANE_PACK, ones_w))
    big = lambda shape, imap: pl.BlockSpec(shape, imap, pipeline_mode=pl.Buffered(1))
    in_specs = [
        pl.BlockSpec((Q_BLOCK, gw), lambda b, g, i: (b * n_qb + i, g)),
        big((None, ncp, HEAD_DIM), lambda b, g, i: (g, b, 0)),
        big((None, HEAD_DIM, ncp), lambda b, g, i: (g, 0, b)),
        big((seq, HEAD_DIM), lambda b, g, i: (b, g)),
        big((seq, HEAD_DIM), lambda b, g, i: (b, NSA_KV_GROUPS + g)),
        big((HEAD_DIM, seq), lambda b, g, i: (g, b)),
        big((HEAD_DIM, seq), lambda b, g, i: (NSA_KV_GROUPS + g, b)),
        pl.BlockSpec((GATE_ROWS, Q_BLOCK), lambda b, g, i: (g, b * n_qb + i)),
        _const_spec(aggt.shape),
        _const_spec(eblk.shape),
        _const_spec(cbias.shape),
        _const_spec(wbias.shape),
        _const_spec(ones_rows.shape),
    ]
    return pl.pallas_call(
        _nsa_kernel, grid=(batch, NSA_KV_GROUPS, n_qb), in_specs=in_specs,
        out_specs=pl.BlockSpec((Q_BLOCK, gw), lambda b, g, i: (b * n_qb + i, g)),
        out_shape=jax.ShapeDtypeStruct((T, NSA_HEADS * HEAD_DIM), F32),
        scratch_shapes=[pltpu.VMEM((n_var, HEAD_DIM + BIAS_SLOTS, NSA_REP * Q_BLOCK), BF16),
                        pltpu.VMEM((KV_CHUNK, NSA_REP * Q_BLOCK), F32)],
        compiler_params=_params(("arbitrary", "arbitrary", "arbitrary")),
        name="nsa",
    )(q, kc, vct, ksw, ksw, vt, vt, gt, aggt, eblk, cbias, wbias, ones_rows)


def _mix_out_kernel(x_ref, on_ref, u_ref, vg_ref, ws_ref, bst_ref, gn_ref, gm_ref, wo_ref,
                    g2_ref, h_ref, y_ref, og_ref):
    tm = x_ref.shape[0]
    row = lax.broadcasted_iota(jnp.int32, (GMLP_CHUNK, GMLP_CHUNK), 0)
    col = lax.broadcasted_iota(jnp.int32, (GMLP_CHUNK, GMLP_CHUNK), 1)
    tril = col <= row
    for h in range(GMLP_GROUPS):
        ws = jnp.where(tril, ws_ref[h], 0.0).astype(BF16)
        bcol = bst_ref[:, h:h + 1]
        cs = slice(h * GMLP_GROUP_DIM, (h + 1) * GMLP_GROUP_DIM)
        for n in range(tm // GMLP_CHUNK):
            rs = slice(n * GMLP_CHUNK, (n + 1) * GMLP_CHUNK)
            mixed = _dot(ws, vg_ref[rs, cs]) + bcol
            og_ref[rs, cs] = u_ref[rs, cs].astype(F32) * mixed

    def rms(v, g):
        return v * lax.rsqrt(jnp.mean(v * v, axis=-1, keepdims=True) + EPS) * g

    half = on_ref.shape[1]
    mix_n = rms(on_ref[...], gn_ref[...]).astype(BF16)
    mix_g = rms(og_ref[...], gm_ref[...]).astype(BF16)
    h1 = x_ref[...] + _dot(mix_n, wo_ref[0:half, :]) + _dot(mix_g, wo_ref[half:, :])
    h_ref[...] = h1
    y_ref[...] = rms(h1, g2_ref[...]).astype(BF16)


def _mix_out(x2, o_nsa, u, vg, ws, bst, gn, gm, wo, g2):
    T, D = x2.shape
    tm = TM_MIX
    half = o_nsa.shape[1]
    tok = lambda w: pl.BlockSpec((tm, w), lambda i: (i, 0))
    in_specs = [tok(D), tok(half), tok(half), tok(half),
                _const_spec(ws.shape), _const_spec(bst.shape),
                _const_spec((1, half)), _const_spec((1, half)),
                _const_spec(wo.shape), _const_spec((1, D))]
    return pl.pallas_call(
        _mix_out_kernel, grid=(T // tm,), in_specs=in_specs,
        out_specs=(tok(D), tok(D)),
        out_shape=(jax.ShapeDtypeStruct((T, D), F32), jax.ShapeDtypeStruct((T, D), BF16)),
        scratch_shapes=[pltpu.VMEM((tm, half), F32)],
        compiler_params=_params(("arbitrary",)),
        name="mix_out",
    )(x2, o_nsa, u, vg, ws, bst, gn, gm, wo, g2)


def _ffn_kernel(y_ref, halo_ref, wg_ref, wu_ref, cwg_ref, cwu_ref, cbg_ref, cbu_ref, wd_ref,
                h_ref, gf_ref, o_ref, ybuf_ref, a_ref, acc_ref, *, tiles_per_seq):
    i = pl.program_id(0)
    j = pl.program_id(1)
    tm = y_ref.shape[0]
    tf = wg_ref.shape[1]
    hr = halo_ref.shape[0]

    @pl.when(j == 0)
    def _():
        keep = jnp.where(i % tiles_per_seq == 0, 0.0, 1.0).astype(BF16)
        ybuf_ref[0:hr] = halo_ref[...] * keep
        ybuf_ref[hr:] = y_ref[...]
        acc_ref[...] = h_ref[...]

    yb = ybuf_ref[...]
    a_ref[:, 0:tf] = _dot(yb, wg_ref[...])
    a_ref[:, tf:] = _dot(yb, wu_ref[...])

    def conv(cols, cw_ref, cb_ref):
        c = cb_ref[...] + cw_ref[CONV_WIDTH - 1:CONV_WIDTH, :] * a_ref[pl.ds(hr, tm), cols]
        for k in range(CONV_WIDTH - 1):
            shift = CONV_WIDTH - 1 - k
            c = c + cw_ref[k:k + 1, :] * a_ref[pl.ds(hr - shift, tm), cols]
        return c

    cg = conv(slice(0, tf), cwg_ref, cbg_ref)
    cu = conv(slice(tf, 2 * tf), cwu_ref, cbu_ref)
    hmid = (cg * jax.nn.sigmoid(cg) * cu).astype(BF16)
    acc_ref[...] += _dot(hmid, wd_ref[...])

    @pl.when(j == pl.num_programs(1) - 1)
    def _():
        hh = acc_ref[...]
        ms = jnp.mean(hh * hh, axis=-1, keepdims=True)
        o_ref[...] = hh * lax.rsqrt(ms + EPS) * gf_ref[...]


def _ffn(y, h1, w_up, conv_w, conv_b, w_down, gf, seq):
    T, D = h1.shape
    dff = w_down.shape[0]
    tm, tf = TM_FFN, TF_FFN
    hr = V7X_BF16_SUBLANE_PACK
    nj = dff // tf
    halo_blocks = tm // hr
    in_specs = [
        pl.BlockSpec((tm, D), lambda i, j: (i, 0)),
        pl.BlockSpec((hr, D), lambda i, j: (jnp.maximum(i * halo_blocks - 1, 0), 0)),
        pl.BlockSpec((D, tf), lambda i, j: (0, j)),
        pl.BlockSpec((D, tf), lambda i, j: (0, nj + j)),
        pl.BlockSpec((CONV_WIDTH, tf), lambda i, j: (0, j)),
        pl.BlockSpec((CONV_WIDTH, tf), lambda i, j: (0, nj + j)),
        pl.BlockSpec((1, tf), lambda i, j: (0, j)),
        pl.BlockSpec((1, tf), lambda i, j: (0, nj + j)),
        pl.BlockSpec((tf, D), lambda i, j: (j, 0)),
        pl.BlockSpec((tm, D), lambda i, j: (i, 0)),
        _const_spec((1, D)),
    ]
    return pl.pallas_call(
        functools.partial(_ffn_kernel, tiles_per_seq=seq // tm),
        grid=(T // tm, nj), in_specs=in_specs,
        out_specs=pl.BlockSpec((tm, D), lambda i, j: (i, 0)),
        out_shape=jax.ShapeDtypeStruct((T, D), F32),
        scratch_shapes=[pltpu.VMEM((tm + hr, D), BF16),
                        pltpu.VMEM((tm + hr, 2 * tf), F32),
                        pltpu.VMEM((tm, D), F32)],
        compiler_params=_params(("arbitrary", "arbitrary")),
        name="ffn",
    )(y, y, w_up, w_up, conv_w, conv_w, conv_b, conv_b, w_down, h1, gf)


def _agg_t(n_sel, ncp, n_cmp):
    c_start = np.arange(ncp) * CMP_STRIDE
    js = np.arange(n_sel)[:, None] * SEL_LEN
    a = (c_start[None, :] < js + SEL_LEN) & (c_start[None, :] + CMP_LEN > js)
    a &= (np.arange(ncp) < n_cmp)[None, :]
    return jnp.asarray(a.astype(np.float32), dtype=BF16)


def _layer(h, positions, norm1_g, w_in, cmp_pe_k, cmp_w_k, cmp_pe_v, cmp_w_v, gmlp_norm_g,
           gmlp_w_s, gmlp_b_s, nsa_out_g, gmlp_out_g, w_out, norm2_g, w_up, conv_w, conv_b,
           w_down, out_g):
    B, S, D = h.shape
    T = B * S
    nq = NSA_HEADS * HEAD_DIM
    kvw = NSA_KV_GROUPS * HEAD_DIM
    x2 = h.reshape(T, D)
    pos2 = positions.reshape(T, 1).astype(jnp.int32)

    half = HEAD_DIM // 2
    inv = ROPE_THETA ** (-2.0 * jnp.arange(half, dtype=F32) / HEAD_DIM)
    inv_full = jnp.concatenate([inv, inv])[None, :]
    sign = jnp.asarray(np.concatenate([-np.ones(half), np.ones(half)])[None, :], dtype=F32)

    kv_w = w_in[:, nq:nq + 6 * kvw].reshape(D, 6, kvw)
    k_cmp_w, v_cmp_w, k_sel_w, v_sel_w, k_win_w, v_win_w = (kv_w[:, i] for i in range(6))
    g_off = nq + 6 * kvw
    n_gate = NSA_HEADS * N_BRANCH
    gate_w = w_in[:, g_off:g_off + n_gate].reshape(D, NSA_KV_GROUPS, NSA_REP * N_BRANCH)
    gate_w = jnp.pad(gate_w, ((0, 0), (0, 0), (0, GATE_ROWS - NSA_REP * N_BRANCH)))
    gate_w = gate_w.reshape(D, NSA_KV_GROUPS * GATE_ROWS)
    u_off = g_off + n_gate
    gw = GMLP_GROUPS * GMLP_GROUP_DIM
    u_w = w_in[:, u_off:u_off + gw]
    v_w = w_in[:, u_off + gw:u_off + 2 * gw]
    wn = jnp.concatenate([w_in[:, :nq], k_sel_w, k_win_w, k_cmp_w, v_cmp_w, u_w, v_w],
                         axis=1).astype(BF16)
    wt = jnp.concatenate([v_sel_w, v_win_w, gate_w], axis=1).T.astype(BF16)

    q, ksw, kvc, u, vg, vt, gt = _inproj(
        x2, pos2, norm1_g.reshape(1, D), inv_full, sign, wn, wt, gmlp_norm_g.reshape(1, gw))

    ncp = S // CMP_STRIDE
    n_cmp = (S - CMP_LEN) // CMP_STRIDE + 1
    hb = CMP_LEN // 2
    kvc2 = kvc.reshape(4, B * ncp, CMP_STRIDE * HEAD_DIM)

    def cmp_weights(w, pe):
        w2 = jnp.concatenate([w[:hb].reshape(hb * HEAD_DIM, HEAD_DIM),
                              w[hb:].reshape(hb * HEAD_DIM, HEAD_DIM)], axis=1).astype(BF16)
        pe8 = jnp.pad(pe.reshape(1, CMP_LEN * HEAD_DIM), ((0, 7), (0, 0))).astype(BF16)
        return w2, pe8, w.reshape(CMP_LEN * HEAD_DIM, HEAD_DIM).astype(BF16)

    posc = jnp.pad(positions[:, CMP_LEN - 1::CMP_STRIDE], ((0, 0), (0, ncp - n_cmp)))
    posc = posc.reshape(B * ncp, 1).astype(jnp.int32)
    kc = _compress(kvc2, *cmp_weights(cmp_w_k, cmp_pe_k), B, rope_args=(posc, inv_full, sign))
    vct = _compress(kvc2, *cmp_weights(cmp_w_v, cmp_pe_v), B)

    o_nsa = _nsa(q, kc, vct, ksw, vt, gt, _agg_t(S // SEL_LEN, ncp, n_cmp), B, S)

    h1, y = _mix_out(x2, o_nsa, u, vg, gmlp_w_s, gmlp_b_s.T, nsa_out_g.reshape(1, nq),
                     gmlp_out_g.reshape(1, gw), w_out.astype(BF16), norm2_g.reshape(1, D))

    out = _ffn(y, h1, w_up.astype(BF16), conv_w, conv_b.reshape(1, -1), w_down.astype(BF16),
               out_g.reshape(1, D), S)
    return out.reshape(B, S, D)


def kernel(x, positions, norm1_g, w_in, cmp_pe_k, cmp_w_k, cmp_pe_v, cmp_w_v, gmlp_norm_g,
           gmlp_w_s, gmlp_b_s, nsa_out_g, gmlp_out_g, w_out, norm2_g, w_up, conv_w, conv_b,
           w_down, final_g):
    depth = norm1_g.shape[0]
    assert depth == 1, "the FFN kernel fuses the final RMSNorm into the only layer"
    return _layer(x, positions, norm1_g[0], w_in[0], cmp_pe_k[0], cmp_w_k[0], cmp_pe_v[0],
                  cmp_w_v[0], gmlp_norm_g[0], gmlp_w_s[0], gmlp_b_s[0], nsa_out_g[0],
                  gmlp_out_g[0], w_out[0], norm2_g[0], w_up[0], conv_w[0], conv_b[0],
                  w_down[0], final_g)
```

```python
import functools

import numpy as np
import jax
import jax.numpy as jnp
from jax import lax
from jax.experimental import pallas as pl
from jax.experimental.pallas import tpu as pltpu

F32 = jnp.float32
BF16 = jnp.bfloat16

HEAD_DIM = 128
NSA_HEADS = 8
NSA_KV_GROUPS = 2
NSA_REP = NSA_HEADS // NSA_KV_GROUPS
N_BRANCH = 3
CMP_LEN = 32
CMP_STRIDE = 16
SEL_LEN = 64
SEL_TOPK = 16
WINDOW = 512
GMLP_GROUP_DIM = 128
GMLP_GROUPS = 8
GMLP_CHUNK = 128
CONV_WIDTH = 3
ROPE_THETA = 10000.0
EPS = 1e-6
Q_BLOCK = 128
NEG_INF = -1e30
LOG2_E = float(np.log2(np.e))
FIXED_SHIFT_LIMIT = 1e30
N_FORCED = 3
TOPK_FIRST_QB = SEL_TOPK * SEL_LEN // Q_BLOCK

V7X_LANES = 128
V7X_BF16_SUBLANE_PACK = 16
V7X_VMEM_LIMIT_BYTES = 56 * 1024 * 1024

TM_PROJ = 512
TM_MIX = 512
TM_FFN = 512
TF_FFN = 512
KV_CHUNK = 512
SEL_UNROLL = 16
ONLINE_UNROLL = 2
BIAS_SLOTS = 128
GATE_ROWS = 16


def _dot(a, b):
    return jnp.dot(a, b, preferred_element_type=F32)


def _dot_nt(a, b):
    return lax.dot_general(a, b, (((1,), (1,)), ((), ())), preferred_element_type=F32)


def _const_spec(shape):
    nd = len(shape)
    return pl.BlockSpec(shape, lambda *_: (0,) * nd, pipeline_mode=pl.Buffered(1))


def _params(semantics):
    return pltpu.CompilerParams(dimension_semantics=semantics,
                                vmem_limit_bytes=V7X_VMEM_LIMIT_BYTES)


def _gelu(x):
    return 0.5 * x * (1.0 + lax.erf(x * (2.0 ** -0.5)))


def _rope(x, cos, sin_signed):
    return x * cos + pltpu.roll(x, HEAD_DIM // 2, 1) * sin_signed


def _inproj_kernel(x_ref, pos_ref, g1_ref, inv_ref, sign_ref, wn_ref, wt_ref, gg_ref,
                   q_ref, ksw_ref, kvc_ref, u_ref, vg_ref, vt_ref, gt_ref):
    x = x_ref[...]
    ms = jnp.mean(x * x, axis=-1, keepdims=True)
    xn = (x * lax.rsqrt(ms + EPS) * g1_ref[...]).astype(BF16)

    ang = pos_ref[...].astype(F32) * inv_ref[...]
    cos = jnp.cos(ang)
    sin_s = jnp.sin(ang) * sign_ref[...]
    scale = HEAD_DIM ** -0.5

    seg_w = 4 * HEAD_DIM

    def seg(i):
        return _dot(xn, wn_ref[:, i * seg_w:(i + 1) * seg_w])

    for i in range(2):
        acc = seg(i)
        for h in range(4):
            qh = _rope(acc[:, h * HEAD_DIM:(h + 1) * HEAD_DIM], cos, sin_s) * scale
            c0 = (i * 4 + h) * HEAD_DIM
            q_ref[:, c0:c0 + HEAD_DIM] = qh.astype(BF16)
    acc = seg(2)
    for h in range(4):
        kh = _rope(acc[:, h * HEAD_DIM:(h + 1) * HEAD_DIM], cos, sin_s)
        ksw_ref[:, h * HEAD_DIM:(h + 1) * HEAD_DIM] = kh.astype(BF16)
    acc = seg(3)
    for s in range(4):
        kvc_ref[s] = acc[:, s * HEAD_DIM:(s + 1) * HEAD_DIM].astype(BF16)
    for i in range(2):
        acc = seg(4 + i)
        u_ref[:, i * seg_w:(i + 1) * seg_w] = _gelu(acc).astype(BF16)
    for i in range(2):
        acc = _gelu(seg(6 + i))
        for h in range(4):
            c0 = i * seg_w + h * GMLP_GROUP_DIM
            vh = acc[:, h * GMLP_GROUP_DIM:(h + 1) * GMLP_GROUP_DIM]
            msv = jnp.mean(vh * vh, axis=-1, keepdims=True)
            vn = vh * lax.rsqrt(msv + EPS) * gg_ref[:, c0:c0 + GMLP_GROUP_DIM]
            vg_ref[:, c0:c0 + GMLP_GROUP_DIM] = vn.astype(BF16)
    rt = _dot_nt(wt_ref[...], xn)
    nv = 4 * HEAD_DIM
    vt_ref[...] = rt[:nv].astype(BF16)
    gt_ref[...] = rt[nv:]


def _inproj(x2, pos2, g1, inv_full, sign, wn, wt, gg):
    T, D = x2.shape
    tm = TM_PROJ
    nt_rows = wt.shape[0]
    grid = (T // tm,)
    tok = lambda w: pl.BlockSpec((tm, w), lambda i: (i, 0))
    out_shape = (
        jax.ShapeDtypeStruct((T, NSA_HEADS * HEAD_DIM), BF16),
        jax.ShapeDtypeStruct((T, 4 * HEAD_DIM), BF16),
        jax.ShapeDtypeStruct((4, T, HEAD_DIM), BF16),
        jax.ShapeDtypeStruct((T, GMLP_GROUPS * GMLP_GROUP_DIM), BF16),
        jax.ShapeDtypeStruct((T, GMLP_GROUPS * GMLP_GROUP_DIM), BF16),
        jax.ShapeDtypeStruct((4 * HEAD_DIM, T), BF16),
        jax.ShapeDtypeStruct((NSA_KV_GROUPS * GATE_ROWS, T), F32),
    )
    out_specs = (
        tok(NSA_HEADS * HEAD_DIM),
        tok(4 * HEAD_DIM),
        pl.BlockSpec((4, tm, HEAD_DIM), lambda i: (0, i, 0)),
        tok(GMLP_GROUPS * GMLP_GROUP_DIM),
        tok(GMLP_GROUPS * GMLP_GROUP_DIM),
        pl.BlockSpec((4 * HEAD_DIM, tm), lambda i: (0, i)),
        pl.BlockSpec((NSA_KV_GROUPS * GATE_ROWS, tm), lambda i: (0, i)),
    )
    in_specs = [
        tok(D),
        pl.BlockSpec((tm, 1), lambda i: (i, 0)),
        _const_spec((1, D)),
        _const_spec((1, HEAD_DIM)),
        _const_spec((1, HEAD_DIM)),
        _const_spec(wn.shape),
        _const_spec((nt_rows, D)),
        _const_spec((1, GMLP_GROUPS * GMLP_GROUP_DIM)),
    ]
    return pl.pallas_call(
        _inproj_kernel, grid=grid, in_specs=in_specs, out_specs=out_specs,
        out_shape=out_shape, compiler_params=_params(("arbitrary",)),
        name="inproj",
    )(x2, pos2, g1, inv_full, sign, wn, wt, gg)


def _compress_body(x_ref, w2_ref, pe_ref, wflat_ref, shift_ref):
    ncp = x_ref.shape[0]
    p = _dot(x_ref[...], w2_ref[...])
    bias = _dot(pe_ref[...], wflat_ref[...])[0:1]
    shift_ref[0:ncp] = p[:, HEAD_DIM:]
    shift_ref[ncp:ncp + 8] = jnp.zeros((8, HEAD_DIM), F32)
    return p[:, :HEAD_DIM] + shift_ref[pl.ds(1, ncp), :] + bias


def _compress_k_kernel(x_ref, w2_ref, pe_ref, wflat_ref, posc_ref, inv_ref, sign_ref,
                       kc_ref, shift_ref):
    kc = _compress_body(x_ref, w2_ref, pe_ref, wflat_ref, shift_ref)
    ang = posc_ref[...].astype(F32) * inv_ref[...]
    kc_ref[...] = _rope(kc, jnp.cos(ang), jnp.sin(ang) * sign_ref[...]).astype(BF16)


def _compress_v_kernel(x_ref, w2_ref, pe_ref, wflat_ref, vct_ref, shift_ref):
    vc = _compress_body(x_ref, w2_ref, pe_ref, wflat_ref, shift_ref)
    vct_ref[...] = vc.T.astype(BF16)


def _compress(kvc, w2, pe8, wflat, batch, rope_args=None):
    ncp = kvc.shape[1] // batch
    kdim = kvc.shape[2]
    grid = (batch, NSA_KV_GROUPS)
    plane0 = 0 if rope_args is not None else NSA_KV_GROUPS
    in_specs = [
        pl.BlockSpec((None, ncp, kdim), lambda b, g: (plane0 + g, b, 0)),
        _const_spec(w2.shape),
        _const_spec(pe8.shape),
        _const_spec(wflat.shape),
    ]
    scratch = [pltpu.VMEM((ncp + 8, HEAD_DIM), F32)]
    if rope_args is not None:
        posc, inv_full, sign = rope_args
        in_specs += [pl.BlockSpec((ncp, 1), lambda b, g: (b, 0)),
                     _const_spec((1, HEAD_DIM)), _const_spec((1, HEAD_DIM))]
        return pl.pallas_call(
            _compress_k_kernel, grid=grid, in_specs=in_specs,
            out_specs=pl.BlockSpec((None, ncp, HEAD_DIM), lambda b, g: (g, b, 0)),
            out_shape=jax.ShapeDtypeStruct((NSA_KV_GROUPS, batch * ncp, HEAD_DIM), BF16),
            scratch_shapes=scratch, compiler_params=_params(("arbitrary", "arbitrary")),
            name="compress_k",
        )(kvc, w2, pe8, wflat, posc, inv_full, sign)
    return pl.pallas_call(
        _compress_v_kernel, grid=grid, in_specs=in_specs,
        out_specs=pl.BlockSpec((None, HEAD_DIM, ncp), lambda b, g: (g, 0, b)),
        out_shape=jax.ShapeDtypeStruct((NSA_KV_GROUPS, HEAD_DIM, batch * ncp), BF16),
        scratch_shapes=scratch, compiler_params=_params(("arbitrary", "arbitrary")),
        name="compress_v",
    )(kvc, w2, pe8, wflat)


def _nsa_kernel(q_ref, kc_ref, vct_ref, ksel_ref, kwin_ref, vselt_ref, vwint_ref, gt_ref,
                aggt_ref, eblk_ref, cbias_ref, wbias_ref, ones_ref, o_ref, qaug_ref, sc_ref):
    qb = pl.program_id(2)
    nq = NSA_REP * Q_BLOCK
    ncp = kc_ref.shape[0]
    nsel = aggt_ref.shape[0]
    n_var = qaug_ref.shape[0]
    chunks_per_var = eblk_ref.shape[0]

    qt_f32 = jnp.concatenate(
        [q_ref[:, r * HEAD_DIM:(r + 1) * HEAD_DIM].astype(F32).T for r in range(NSA_REP)], axis=1)
    qt = qt_f32.astype(BF16)
    qt2 = (qt_f32 * LOG2_E).astype(BF16)
    lane4 = lax.broadcasted_iota(jnp.int32, (1, nq), 1)
    t4 = qb * Q_BLOCK + (lane4 & (Q_BLOCK - 1))

    def pv_aug(vt, pr):
        vt_aug = jnp.concatenate([vt, ones_ref[:, 0:vt.shape[1]]], axis=0)
        return _dot(vt_aug, pr)

    def pv_and_rowsum(vt, pr):
        r_aug = pv_aug(vt, pr)
        return r_aug[0:HEAD_DIM], r_aug[HEAD_DIM:HEAD_DIM + 1]

    def compressed_window_select(nc, ns):
        s = _dot(kc_ref[0:nc, :], qt)
        wk = WINDOW + Q_BLOCK
        w0_ = pl.multiple_of(jnp.maximum(qb * Q_BLOCK - WINDOW, 0), Q_BLOCK)
        sw = _dot(kwin_ref[pl.ds(w0_, wk), :], qt2)

        cb0 = pl.multiple_of(ncp - qb * (Q_BLOCK // CMP_STRIDE), Q_BLOCK // CMP_STRIDE)
        s = s + jnp.concatenate([cbias_ref[pl.ds(cb0, nc), :]] * NSA_REP, axis=1)
        m = jnp.max(s, axis=0, keepdims=True)
        e = jnp.exp(s - m)
        oc, l = pv_and_rowsum(vct_ref[:, 0:nc], e.astype(BF16))
        inv_l = jnp.where(t4 >= CMP_LEN - 1, 1.0 / l, 0.0)
        o_c = oc * inv_l

        ps = e[:, 0:Q_BLOCK] * inv_l[:, 0:Q_BLOCK]
        for r in range(1, NSA_REP):
            cs = slice(r * Q_BLOCK, (r + 1) * Q_BLOCK)
            ps = ps + e[:, cs] * inv_l[:, cs]
        ps_hi = ps.astype(BF16)
        ps_lo = (ps - ps_hi.astype(F32)).astype(BF16)
        aggt = aggt_ref[0:ns, 0:nc]
        imp = _dot(aggt, ps_hi) + _dot(aggt, ps_lo)

        wv = jnp.minimum(qb, WINDOW // Q_BLOCK)
        sw = sw + jnp.concatenate([wbias_ref[wv]] * NSA_REP, axis=1)
        mw = jnp.max(sw, axis=0, keepdims=True)
        ew = jnp.exp2(sw - mw)
        ow, lw = pv_and_rowsum(vwint_ref[:, pl.ds(w0_, wk)], ew.astype(BF16))
        o_w = ow * (1.0 / lw)

        jj = lax.broadcasted_iota(jnp.int32, (ns, Q_BLOCK), 0)
        t1 = qb * Q_BLOCK + lax.broadcasted_iota(jnp.int32, (1, Q_BLOCK), 1)
        cur = t1 >> (SEL_LEN.bit_length() - 1)
        valid = jj <= cur

        forced = (jj == 0) | (jj == cur) | (jj == cur - 1)
        cand = valid & jnp.logical_not(forced)
        lowest = -1.0
        w = jnp.where(cand, imp, lowest)
        jf = jj.astype(F32)
        for _ in range(SEL_TOPK - N_FORCED):
            mx = jnp.max(w, axis=0, keepdims=True)
            idx = jnp.min(jnp.where(w == mx, jf, float(ns)), axis=0, keepdims=True)
            w = jnp.where(jf == idx, lowest, w)
        picked = forced | (cand & (w == lowest))
        sb = jnp.where(qb < TOPK_FIRST_QB,
                       jnp.where(valid, 0.0, NEG_INF), jnp.where(picked, 0.0, NEG_INF))
        if ns < nsel:
            sb = jnp.concatenate([sb, jnp.full((nsel - ns, Q_BLOCK), NEG_INF, F32)], axis=0)
        return o_c, sb, o_w

    def sized_variant(div):
        aligned = (ncp // div) % V7X_LANES == 0 and (nsel // div) % V7X_BF16_SUBLANE_PACK == 0
        return (lambda: compressed_window_select(ncp // div, nsel // div)) if aligned else None

    full, half, quarter = sized_variant(1), sized_variant(2), sized_variant(4)
    n_qb = nsel * SEL_LEN // Q_BLOCK
    if half is None:
        o_cmp, selb, o_win = full()
    elif quarter is None:
        o_cmp, selb, o_win = lax.cond(qb < n_qb // 2, half, full)
    else:
        o_cmp, selb, o_win = lax.cond(
            qb < n_qb // 2, lambda: lax.cond(qb < n_qb // 4, quarter, half), full)

    selb4 = jnp.concatenate([selb] * NSA_REP, axis=1)
    pad_rows = n_var * BIAS_SLOTS - nsel
    if pad_rows:
        selb4 = jnp.concatenate([selb4, jnp.zeros((pad_rows, nq), F32)], axis=0)
    for v in range(n_var):
        qaug_ref[v, 0:HEAD_DIM, :] = qt2
        qaug_ref[v, HEAD_DIM:, :] = selb4[v * BIAS_SLOTS:(v + 1) * BIAS_SLOTS].astype(BF16)

    k_idx = lax.broadcasted_iota(jnp.int32, (KV_CHUNK, 1), 0)

    def sel_scores(c):
        k0 = pl.multiple_of(c * KV_CHUNK, KV_CHUNK)
        k_aug = jnp.concatenate([ksel_ref[pl.ds(k0, KV_CHUNK), :], eblk_ref[c % chunks_per_var]],
                                axis=1)
        return _dot(k_aug, qaug_ref[c // chunks_per_var])

    def causal_mask(c, sc):
        return jnp.where((c * KV_CHUNK + k_idx) <= t4, sc, NEG_INF)

    def sel_pv(c, pc):
        k0 = pl.multiple_of(c * KV_CHUNK, KV_CHUNK)
        return pv_aug(vselt_ref[:, pl.ds(k0, KV_CHUNK)], pc.astype(BF16))

    def online_update(c, sc, carry, causal):
        m_i, acc = carry
        if causal:
            sc = causal_mask(c, sc)
        m_new = jnp.maximum(m_i, jnp.max(sc, axis=0, keepdims=True))
        alpha = jnp.exp2(m_i - m_new)
        return m_new, alpha * acc + sel_pv(c, jnp.exp2(sc - m_new))

    def sel_chunks(c0, carry, update, n, last):
        sc = sc_ref[...]
        for u in range(n):
            nxt = None if last else sel_scores(c0 + u + 1)
            carry = update(c0 + u, sc, carry, last)
            sc = nxt
        if not last:
            sc_ref[...] = sc
        return carry

    def sel_branch(update, init, sc_first, unroll):
        diag = qb // (KV_CHUNK // Q_BLOCK)
        n_groups = diag // unroll
        sc_ref[...] = sc_first
        carry = lax.fori_loop(
            0, n_groups, lambda gi, cy: sel_chunks(gi * unroll, cy, update, unroll, False), init)
        c = n_groups * unroll
        size = unroll // 2
        while size >= 1:
            take = ((diag - c) & size) != 0
            carry = lax.cond(
                take, lambda cy, c=c, size=size: sel_chunks(c, cy, update, size, False),
                lambda cy: cy, carry)
            c = c + jnp.where(take, size, 0)
            size //= 2
        return sel_chunks(diag, carry, update, 1, True)

    acc0 = jnp.zeros((HEAD_DIM + ones_ref.shape[0], nq), F32)

    sc0 = sel_scores(0)
    m0 = sc0[0:1, :]

    def fixed_shift_update(c, sc, acc, causal):
        if causal:
            sc = causal_mask(c, sc)
        return acc + sel_pv(c, jnp.exp2(sc - m0))

    acc_fast = sel_branch(fixed_shift_update, acc0, sc0, SEL_UNROLL)

    gate = jax.nn.sigmoid(gt_ref[...])

    def write_output(acc_sel):
        o_sel = acc_sel[0:HEAD_DIM] * (1.0 / acc_sel[HEAD_DIM:HEAD_DIM + 1])
        for r in range(NSA_REP):
            cs = slice(r * Q_BLOCK, (r + 1) * Q_BLOCK)
            g0 = gate[N_BRANCH * r + 0:N_BRANCH * r + 1, :]
            g1 = gate[N_BRANCH * r + 1:N_BRANCH * r + 2, :]
            g2 = gate[N_BRANCH * r + 2:N_BRANCH * r + 3, :]
            ot = g0 * o_cmp[:, cs] + g1 * o_sel[:, cs] + g2 * o_win[:, cs]
            o_ref[:, r * HEAD_DIM:(r + 1) * HEAD_DIM] = ot.T

    write_output(acc_fast)

    out_of_range = jnp.logical_not(jnp.max(jnp.abs(acc_fast)) < FIXED_SHIFT_LIMIT)

    @pl.when(out_of_range)
    def _():
        init = (jnp.full((1, nq), NEG_INF, F32), acc0)
        write_output(sel_branch(online_update, init, sel_scores(0), ONLINE_UNROLL)[1])


def _block_onehot(n_chunks):
    per_chunk = KV_CHUNK // SEL_LEN
    k = np.arange(KV_CHUNK)[None, :, None]
    e = np.arange(n_chunks)[:, None, None]
    x = np.arange(BIAS_SLOTS)[None, None, :]
    return jnp.asarray((x == e * per_chunk + k // SEL_LEN).astype(np.float32), dtype=BF16)


def _cmp_bias(ncp):
    n_rel = np.arange(2 * ncp)[:, None] - ncp
    q_rel = np.arange(Q_BLOCK)[None, :]
    vis = n_rel * CMP_STRIDE + (CMP_LEN - 1) <= q_rel
    return jnp.asarray(np.where(vis, 0.0, NEG_INF), dtype=F32)


def _win_bias():
    n_var = WINDOW // Q_BLOCK
    base = np.minimum(np.arange(n_var + 1) * Q_BLOCK, WINDOW)[:, None, None]
    d = base + np.arange(Q_BLOCK)[None, None, :] - np.arange(WINDOW + Q_BLOCK)[None, :, None]
    return jnp.asarray(np.where((d >= 0) & (d < WINDOW), 0.0, NEG_INF), dtype=F32)


def _nsa(q, kc, vct, ksw, vt, gt, aggt, batch, seq):
    T = q.shape[0]
    n_qb = seq // Q_BLOCK
    ncp = kc.shape[1] // batch
    nsel = aggt.shape[0]
    gw = NSA_REP * HEAD_DIM
    assert seq % KV_CHUNK == 0
    n_var = pl.cdiv(nsel, BIAS_SLOTS)
    eblk = _block_onehot(min(BIAS_SLOTS * SEL_LEN, seq) // KV_CHUNK)
    cbias = _cmp_bias(ncp)
    wbias = _win_bias()
    ones_w = max(ncp, WINDOW + Q_BLOCK, KV_CHUNK)
    ones_rows = jnp.asarray(np.arange(V7X_BF16_SUBLANE_PACK)[:, None] == 0, dtype=BF16)
    ones_rows = jnp.broadcast_to(ones_rows, (V7X_BF16_SUBLANE_PACK, ones_w))
    big = lambda shape, imap: pl.BlockSpec(shape, imap, pipeline_mode=pl.Buffered(1))
    in_specs = [
        pl.BlockSpec((Q_BLOCK, gw), lambda b, g, i: (b * n_qb + i, g)),
        big((None, ncp, HEAD_DIM), lambda b, g, i: (g, b, 0)),
        big((None, HEAD_DIM, ncp), lambda b, g, i: (g, 0, b)),
        big((seq, HEAD_DIM), lambda b, g, i: (b, g)),
        big((seq, HEAD_DIM), lambda b, g, i: (b, NSA_KV_GROUPS + g)),
        big((HEAD_DIM, seq), lambda b, g, i: (g, b)),
        big((HEAD_DIM, seq), lambda b, g, i: (NSA_KV_GROUPS + g, b)),
        pl.BlockSpec((GATE_ROWS, Q_BLOCK), lambda b, g, i: (g, b * n_qb + i)),
        _const_spec(aggt.shape),
        _const_spec(eblk.shape),
        _const_spec(cbias.shape),
        _const_spec(wbias.shape),
        _const_spec(ones_rows.shape),
    ]
    return pl.pallas_call(
        _nsa_kernel, grid=(batch, NSA_KV_GROUPS, n_qb), in_specs=in_specs,
        out_specs=pl.BlockSpec((Q_BLOCK, gw), lambda b, g, i: (b * n_qb + i, g)),
        out_shape=jax.ShapeDtypeStruct((T, NSA_HEADS * HEAD_DIM), F32),
        scratch_shapes=[pltpu.VMEM((n_var, HEAD_DIM + BIAS_SLOTS, NSA_REP * Q_BLOCK), BF16),
                        pltpu.VMEM((KV_CHUNK, NSA_REP * Q_BLOCK), F32)],
        compiler_params=_params(("arbitrary", "arbitrary", "arbitrary")),
        name="nsa",
    )(q, kc, vct, ksw, ksw, vt, vt, gt, aggt, eblk, cbias, wbias, ones_rows)


def _mix_out_kernel(x_ref, on_ref, u_ref, vg_ref, ws_ref, bst_ref, gn_ref, gm_ref, wo_ref,
                    g2_ref, h_ref, y_ref, og_ref):
    tm = x_ref.shape[0]
    row = lax.broadcasted_iota(jnp.int32, (GMLP_CHUNK, GMLP_CHUNK), 0)
    col = lax.broadcasted_iota(jnp.int32, (GMLP_CHUNK, GMLP_CHUNK), 1)
    tril = col <= row
    for h in range(GMLP_GROUPS):
        ws = jnp.where(tril, ws_ref[h], 0.0).astype(BF16)
        bcol = bst_ref[:, h:h + 1]
        cs = slice(h * GMLP_GROUP_DIM, (h + 1) * GMLP_GROUP_DIM)
        for n in range(tm // GMLP_CHUNK):
            rs = slice(n * GMLP_CHUNK, (n + 1) * GMLP_CHUNK)
            mixed = _dot(ws, vg_ref[rs, cs]) + bcol
            og_ref[rs, cs] = u_ref[rs, cs].astype(F32) * mixed

    def rms(v, g):
        return v * lax.rsqrt(jnp.mean(v * v, axis=-1, keepdims=True) + EPS) * g

    half = on_ref.shape[1]
    mix_n = rms(on_ref[...], gn_ref[...]).astype(BF16)
    mix_g = rms(og_ref[...], gm_ref[...]).astype(BF16)
    h1 = x_ref[...] + _dot(mix_n, wo_ref[0:half, :]) + _dot(mix_g, wo_ref[half:, :])
    h_ref[...] = h1
    y_ref[...] = rms(h1, g2_ref[...]).astype(BF16)


def _mix_out(x2, o_nsa, u, vg, ws, bst, gn, gm, wo, g2):
    T, D = x2.shape
    tm = TM_MIX
    half = o_nsa.shape[1]
    tok = lambda w: pl.BlockSpec((tm, w), lambda i: (i, 0))
    in_specs = [tok(D), tok(half), tok(half), tok(half),
                _const_spec(ws.shape), _const_spec(bst.shape),
                _const_spec((1, half)), _const_spec((1, half)),
                _const_spec(wo.shape), _const_spec((1, D))]
    return pl.pallas_call(
        _mix_out_kernel, grid=(T // tm,), in_specs=in_specs,
        out_specs=(tok(D), tok(D)),
        out_shape=(jax.ShapeDtypeStruct((T, D), F32), jax.ShapeDtypeStruct((T, D), BF16)),
        scratch_shapes=[pltpu.VMEM((tm, half), F32)],
        compiler_params=_params(("arbitrary",)),
        name="mix_out",
    )(x2, o_nsa, u, vg, ws, bst, gn, gm, wo, g2)


def _ffn_kernel(y_ref, halo_ref, wg_ref, wu_ref, cwg_ref, cwu_ref, cbg_ref, cbu_ref, wd_ref,
                h_ref, gf_ref, o_ref, ybuf_ref, a_ref, acc_ref, *, tiles_per_seq):
    i = pl.program_id(0)
    j = pl.program_id(1)
    tm = y_ref.shape[0]
    tf = wg_ref.shape[1]
    hr = halo_ref.shape[0]

    @pl.when(j == 0)
    def _():
        keep = jnp.where(i % tiles_per_seq == 0, 0.0, 1.0).astype(BF16)
        ybuf_ref[0:hr] = halo_ref[...] * keep
        ybuf_ref[hr:] = y_ref[...]
        acc_ref[...] = h_ref[...]

    yb = ybuf_ref[...]
    a_ref[:, 0:tf] = _dot(yb, wg_ref[...])
    a_ref[:, tf:] = _dot(yb, wu_ref[...])

    def conv(cols, cw_ref, cb_ref):
        c = cb_ref[...] + cw_ref[CONV_WIDTH - 1:CONV_WIDTH, :] * a_ref[pl.ds(hr, tm), cols]
        for k in range(CONV_WIDTH - 1):
            shift = CONV_WIDTH - 1 - k
            c = c + cw_ref[k:k + 1, :] * a_ref[pl.ds(hr - shift, tm), cols]
        return c

    cg = conv(slice(0, tf), cwg_ref, cbg_ref)
    cu = conv(slice(tf, 2 * tf), cwu_ref, cbu_ref)
    hmid = (cg * jax.nn.sigmoid(cg) * cu).astype(BF16)
    acc_ref[...] += _dot(hmid, wd_ref[...])

    @pl.when(j == pl.num_programs(1) - 1)
    def _():
        hh = acc_ref[...]
        ms = jnp.mean(hh * hh, axis=-1, keepdims=True)
        o_ref[...] = hh * lax.rsqrt(ms + EPS) * gf_ref[...]


def _ffn(y, h1, w_up, conv_w, conv_b, w_down, gf, seq):
    T, D = h1.shape
    dff = w_down.shape[0]
    tm, tf = TM_FFN, TF_FFN
    hr = V7X_BF16_SUBLANE_PACK
    nj = dff // tf
    halo_blocks = tm // hr
    in_specs = [
        pl.BlockSpec((tm, D), lambda i, j: (i, 0)),
        pl.BlockSpec((hr, D), lambda i, j: (jnp.maximum(i * halo_blocks - 1, 0), 0)),
        pl.BlockSpec((D, tf), lambda i, j: (0, j)),
        pl.BlockSpec((D, tf), lambda i, j: (0, nj + j)),
        pl.BlockSpec((CONV_WIDTH, tf), lambda i, j: (0, j)),
        pl.BlockSpec((CONV_WIDTH, tf), lambda i, j: (0, nj + j)),
        pl.BlockSpec((1, tf), lambda i, j: (0, j)),
        pl.BlockSpec((1, tf), lambda i, j: (0, nj + j)),
        pl.BlockSpec((tf, D), lambda i, j: (j, 0)),
        pl.BlockSpec((tm, D), lambda i, j: (i, 0)),
        _const_spec((1, D)),
    ]
    return pl.pallas_call(
        functools.partial(_ffn_kernel, tiles_per_seq=seq // tm),
        grid=(T // tm, nj), in_specs=in_specs,
        out_specs=pl.BlockSpec((tm, D), lambda i, j: (i, 0)),
        out_shape=jax.ShapeDtypeStruct((T, D), F32),
        scratch_shapes=[pltpu.VMEM((tm + hr, D), BF16),
                        pltpu.VMEM((tm + hr, 2 * tf), F32),
                        pltpu.VMEM((tm, D), F32)],
        compiler_params=_params(("arbitrary", "arbitrary")),
        name="ffn",
    )(y, y, w_up, w_up, conv_w, conv_w, conv_b, conv_b, w_down, h1, gf)


def _agg_t(n_sel, ncp, n_cmp):
    c_start = np.arange(ncp) * CMP_STRIDE
    js = np.arange(n_sel)[:, None] * SEL_LEN
    a = (c_start[None, :] < js + SEL_LEN) & (c_start[None, :] + CMP_LEN > js)
    a &= (np.arange(ncp) < n_cmp)[None, :]
    return jnp.asarray(a.astype(np.float32), dtype=BF16)


def _layer(h, positions, norm1_g, w_in, cmp_pe_k, cmp_w_k, cmp_pe_v, cmp_w_v, gmlp_norm_g,
           gmlp_w_s, gmlp_b_s, nsa_out_g, gmlp_out_g, w_out, norm2_g, w_up, conv_w, conv_b,
           w_down, out_g):
    B, S, D = h.shape
    T = B * S
    nq = NSA_HEADS * HEAD_DIM
    kvw = NSA_KV_GROUPS * HEAD_DIM
    x2 = h.reshape(T, D)
    pos2 = positions.reshape(T, 1).astype(jnp.int32)

    half = HEAD_DIM // 2
    inv = ROPE_THETA ** (-2.0 * jnp.arange(half, dtype=F32) / HEAD_DIM)
    inv_full = jnp.concatenate([inv, inv])[None, :]
    sign = jnp.asarray(np.concatenate([-np.ones(half), np.ones(half)])[None, :], dtype=F32)

    kv_w = w_in[:, nq:nq + 6 * kvw].reshape(D, 6, kvw)
    k_cmp_w, v_cmp_w, k_sel_w, v_sel_w, k_win_w, v_win_w = (kv_w[:, i] for i in range(6))
    g_off = nq + 6 * kvw
    n_gate = NSA_HEADS * N_BRANCH
    gate_w = w_in[:, g_off:g_off + n_gate].reshape(D, NSA_KV_GROUPS, NSA_REP * N_BRANCH)
    gate_w = jnp.pad(gate_w, ((0, 0), (0, 0), (0, GATE_ROWS - NSA_REP * N_BRANCH)))
    gate_w = gate_w.reshape(D, NSA_KV_GROUPS * GATE_ROWS)
    u_off = g_off + n_gate
    gw = GMLP_GROUPS * GMLP_GROUP_DIM
    u_w = w_in[:, u_off:u_off + gw]
    v_w = w_in[:, u_off + gw:u_off + 2 * gw]
    wn = jnp.concatenate([w_in[:, :nq], k_sel_w, k_win_w, k_cmp_w, v_cmp_w, u_w, v_w],
                         axis=1).astype(BF16)
    wt = jnp.concatenate([v_sel_w, v_win_w, gate_w], axis=1).T.astype(BF16)

    q, ksw, kvc, u, vg, vt, gt = _inproj(
        x2, pos2, norm1_g.reshape(1, D), inv_full, sign, wn, wt, gmlp_norm_g.reshape(1, gw))

    ncp = S // CMP_STRIDE
    n_cmp = (S - CMP_LEN) // CMP_STRIDE + 1
    hb = CMP_LEN // 2
    kvc2 = kvc.reshape(4, B * ncp, CMP_STRIDE * HEAD_DIM)

    def cmp_weights(w, pe):
        w2 = jnp.concatenate([w[:hb].reshape(hb * HEAD_DIM, HEAD_DIM),
                              w[hb:].reshape(hb * HEAD_DIM, HEAD_DIM)], axis=1).astype(BF16)
        pe8 = jnp.pad(pe.reshape(1, CMP_LEN * HEAD_DIM), ((0, 7), (0, 0))).astype(BF16)
        return w2, pe8, w.reshape(CMP_LEN * HEAD_DIM, HEAD_DIM).astype(BF16)

    posc = jnp.pad(positions[:, CMP_LEN - 1::CMP_STRIDE], ((0, 0), (0, ncp - n_cmp)))
    posc = posc.reshape(B * ncp, 1).astype(jnp.int32)
    kc = _compress(kvc2, *cmp_weights(cmp_w_k, cmp_pe_k), B, rope_args=(posc, inv_full, sign))
    vct = _compress(kvc2, *cmp_weights(cmp_w_v, cmp_pe_v), B)

    o_nsa = _nsa(q, kc, vct, ksw, vt, gt, _agg_t(S // SEL_LEN, ncp, n_cmp), B, S)

    h1, y = _mix_out(x2, o_nsa, u, vg, gmlp_w_s, gmlp_b_s.T, nsa_out_g.reshape(1, nq),
                     gmlp_out_g.reshape(1, gw), w_out.astype(BF16), norm2_g.reshape(1, D))

    out = _ffn(y, h1, w_up.astype(BF16), conv_w, conv_b.reshape(1, -1), w_down.astype(BF16),
               out_g.reshape(1, D), S)
    return out.reshape(B, S, D)


def kernel(x, positions, norm1_g, w_in, cmp_pe_k, cmp_w_k, cmp_pe_v, cmp_w_v, gmlp_norm_g,
           gmlp_w_s, gmlp_b_s, nsa_out_g, gmlp_out_g, w_out, norm2_g, w_up, conv_w, conv_b,
           w_down, final_g):
    depth = norm1_g.shape[0]
    assert depth == 1, "the FFN kernel fuses the final RMSNorm into the only layer"
    return _layer(x, positions, norm1_g[0], w_in[0], cmp_pe_k[0], cmp_w_k[0], cmp_pe_v[0],
                  cmp_w_v[0], gmlp_norm_g[0], gmlp_w_s[0], gmlp_b_s[0], nsa_out_g[0],
                  gmlp_out_g[0], w_out[0], norm2_g[0], w_up[0], conv_w[0], conv_b[0],
                  w_down[0], final_g)
```
